```python
import math
import jax, jax.numpy as jnp
from jax import lax
import numpy as np

D_MODEL = 1024
BATCH = 8
SEQ = 2048
DEPTH = 4
DEC_BATCH = 128
DEC_SEQ = 4
PAST_LEN = 16384
PAGE_SIZE = 128

D_PLE = 256
D_FF = 2048
S5_WIDTH = 512
S5_GROUP = 16
S5_GROUPS = S5_WIDTH // S5_GROUP
S5_STATE = 64
MLSTM_WIDTH = 1024
MLSTM_HEADS = 4
MLSTM_HEAD_DIM = MLSTM_WIDTH // MLSTM_HEADS
MLSTM_CHUNK = 128
CONV_WIDTH = 4
N_BRANCHES = 2
IN_COLS = S5_WIDTH + 4 * MLSTM_WIDTH + 2 * MLSTM_HEADS + N_BRANCHES * D_MODEL
EPS = 1e-6

kernel_name = 'hybrid_s5_mlstm_macaron_step'


def _rmsnorm(x, g):
    xf = x.astype(jnp.float32)
    y = xf * lax.rsqrt(jnp.mean(xf * xf, axis=-1, keepdims=True) + EPS) * g.astype(jnp.float32)
    return y.astype(x.dtype)


def _swiglu(x, w_gate, w_up, w_down):
    return (jax.nn.silu(x @ w_gate) * (x @ w_up)) @ w_down


def _combine(a, b):
    ar, ai, br, bi = a
    cr, ci, dr, di = b
    return (cr * ar - ci * ai, cr * ai + ci * ar, cr * br - ci * bi + dr, cr * bi + ci * br + di)


def _s5(u, h0_re, h0_im, lam_re, lam_im, log_dt, b_re, b_im, c_re, c_im, d_skip):
    f32 = jnp.float32
    Bsz, S, _ = u.shape
    uf = u.astype(f32).reshape(Bsz, S, S5_GROUPS, S5_GROUP)
    lre = lam_re.astype(f32)
    lim = lam_im.astype(f32)
    dt = jnp.exp(log_dt.astype(f32))[:, None]
    mag = jnp.exp(lre * dt)
    abar_re = mag * jnp.cos(lim * dt)
    abar_im = mag * jnp.sin(lim * dt)
    den = lre * lre + lim * lim
    pr = abar_re - 1.0
    w_re = (pr * lre + abar_im * lim) / den
    w_im = (abar_im * lre - pr * lim) / den
    br_ = b_re.astype(f32)
    bi_ = b_im.astype(f32)
    bb_re = w_re[..., None] * br_ - w_im[..., None] * bi_
    bb_im = w_re[..., None] * bi_ + w_im[..., None] * br_
    bu_re = jnp.einsum('bsgh,gnh->bsgn', uf, bb_re)
    bu_im = jnp.einsum('bsgh,gnh->bsgn', uf, bb_im)
    h0r = h0_re.astype(f32)
    h0i = h0_im.astype(f32)
    bu_re = bu_re.at[:, 0].add(abar_re * h0r - abar_im * h0i)
    bu_im = bu_im.at[:, 0].add(abar_re * h0i + abar_im * h0r)
    ar = jnp.broadcast_to(abar_re, bu_re.shape)
    ai = jnp.broadcast_to(abar_im, bu_re.shape)
    _, _, hr, hi = lax.associative_scan(_combine, (ar, ai, bu_re, bu_im), axis=1)
    y = (jnp.einsum('bsgn,ghn->bsgh', hr, c_re.astype(f32))
         - jnp.einsum('bsgn,ghn->bsgh', hi, c_im.astype(f32))
         + d_skip.astype(f32).reshape(S5_GROUPS, S5_GROUP) * uf)
    return y.reshape(Bsz, S, S5_WIDTH), hr[:, -1], hi[:, -1]


def _causal_conv(x_raw, buf, w, b):
    S = x_raw.shape[1]
    xp = jnp.concatenate([buf.astype(x_raw.dtype), x_raw], axis=1)
    out = b.astype(x_raw.dtype)
    for j in range(CONV_WIDTH):
        out = out + xp[:, j:j + S] * w[j].astype(x_raw.dtype)
    return out, xp[:, S:]


def _mlstm(q, k, v, log_i, log_f, c0, n0, m0):
    Bsz, S, H, DH = q.shape
    L = math.gcd(S, MLSTM_CHUNK)
    NC = S // L

    def chunks(t):
        t = t.reshape((Bsz, NC, L) + t.shape[2:])
        return jnp.moveaxis(jnp.moveaxis(t, 1, 0), 3, 2)

    tril = jnp.tril(jnp.ones((L, L), dtype=bool))

    def step(carry, inp):
        C, n, m = carry
        qc, kc, vc, li, lf = inp
        b = jnp.cumsum(lf, axis=-1)
        dmat = b[..., :, None] - b[..., None, :] + li[..., None, :]
        dmat = jnp.where(tril, dmat, -jnp.inf)
        inter = b + m[..., None]
        m_t = jnp.maximum(inter, jnp.max(dmat, axis=-1))
        w = jnp.exp(dmat - m_t[..., None])
        scores = jnp.einsum('bhtd,bhsd->bhts', qc, kc) * w
        a = jnp.exp(inter - m_t)
        num = (jnp.einsum('bhts,bhsd->bhtd', scores, vc)
               + a[..., None] * jnp.einsum('bhtk,bhkv->bhtv', qc, C))
        nq = jnp.sum(scores, axis=-1) + a * jnp.einsum('bhtk,bhk->bht', qc, n)
        h = num / jnp.maximum(jnp.abs(nq), jnp.exp(-m_t))[..., None]
        b_last = b[..., -1]
        g = b_last[..., None] - b + li
        m_new = jnp.maximum(b_last + m, jnp.max(g, axis=-1))
        wk = jnp.exp(g - m_new[..., None])
        decay = jnp.exp(b_last + m - m_new)
        C_new = decay[..., None, None] * C + jnp.einsum('bhs,bhsk,bhsv->bhkv', wk, kc, vc)
        n_new = decay[..., None] * n + jnp.einsum('bhs,bhsk->bhk', wk, kc)
        return (C_new, n_new, m_new), h

    carry0 = (c0.astype(jnp.float32), n0.astype(jnp.float32), m0.astype(jnp.float32))
    (C, n, m), hs = lax.scan(step, carry0, (chunks(q), chunks(k), chunks(v), chunks(log_i), chunks(log_f)))
    h = jnp.transpose(hs, (1, 0, 3, 2, 4)).reshape(Bsz, S, H, DH)
    return h, C, n, m


def _mixer(h, s5_re0, s5_im0, c0, n0, m0, conv0, lw):
    f32 = jnp.float32
    Bsz, S, _ = h.shape
    sizes = (S5_WIDTH, 2 * MLSTM_WIDTH, MLSTM_WIDTH, MLSTM_WIDTH, MLSTM_HEADS, MLSTM_HEADS, D_MODEL, D_MODEL)
    cuts = [int(c) for c in np.cumsum(sizes)[:-1]]
    u, qk_raw, v, o, ig, fg, gate_s5, gate_m = jnp.split(h @ lw['w_in'], cuts, axis=-1)
    y_s5, s5_re, s5_im = _s5(u, s5_re0, s5_im0, lw['s5_lambda_re'], lw['s5_lambda_im'], lw['s5_log_dt'],
                             lw['s5_b_re'], lw['s5_b_im'], lw['s5_c_re'], lw['s5_c_im'], lw['s5_d'])
    y_s5 = jax.nn.gelu(y_s5.astype(h.dtype))
    y_s5 = y_s5 * jax.nn.sigmoid(y_s5 @ lw['s5_w_glu'])
    qk, conv_new = _causal_conv(qk_raw, conv0, lw['conv_w'], lw['conv_b'])
    q, k = jnp.split(jax.nn.silu(qk), 2, axis=-1)
    shp = (Bsz, S, MLSTM_HEADS, MLSTM_HEAD_DIM)
    q = q.reshape(shp).astype(f32) * (MLSTM_HEAD_DIM ** -0.5)
    k = k.reshape(shp).astype(f32)
    v = v.reshape(shp).astype(f32)
    log_i = ig.astype(f32) + lw['b_igate'].astype(f32)
    log_f = jax.nn.log_sigmoid(fg.astype(f32) + lw['b_fgate'].astype(f32))
    hm, c_new, n_new, m_new = _mlstm(q, k, v, log_i, log_f, c0, n0, m0)
    hm = hm * lax.rsqrt(jnp.mean(hm * hm, axis=-1, keepdims=True) + EPS)
    hm = (hm.reshape(Bsz, S, MLSTM_WIDTH) * lw['g_mhead'].astype(f32)).astype(h.dtype) * jax.nn.sigmoid(o)
    merged = (jax.nn.sigmoid(gate_s5) * (y_s5 @ lw['w_s5_up'])
              + jax.nn.sigmoid(gate_m) * (hm @ lw['w_m_up']))
    return merged @ lw['w_out'], (s5_re, s5_im, c_new, n_new, m_new, conv_new)


def _layer(x, p, s5_re0, s5_im0, c0, n0, m0, conv0, lw):
    x = x + 0.5 * _swiglu(_rmsnorm(x, lw['g_ffn1']), lw['w1_gate'], lw['w1_up'], lw['w1_down'])
    mix, new_state = _mixer(_rmsnorm(x, lw['g_mix']), s5_re0, s5_im0, c0, n0, m0, conv0, lw)
    x = x + mix
    x = x + 0.5 * _swiglu(_rmsnorm(x, lw['g_ffn2']), lw['w2_gate'], lw['w2_up'], lw['w2_down'])
    gate = jax.nn.sigmoid(_rmsnorm(x, lw['g_ple']) @ lw['w_ple_gate'])
    x = x + (p.astype(x.dtype) @ lw['w_ple']) * gate
    return x, new_state


def setup_inputs(seed: int = 0) -> dict:
    key = jax.random.key(seed)
    ks = jax.random.split(key, 48)
    keys = [ks[i] for i in range(48)]

    def nk():
        return keys.pop(0)

    def nrm(shape, scale=1.0):
        return jax.random.normal(nk(), shape, jnp.float32) * scale

    def gain(shape):
        return 1.0 + nrm(shape, 0.02)

    L = DEPTH
    G, N = S5_GROUPS, S5_STATE
    H, DH = MLSTM_HEADS, MLSTM_HEAD_DIM
    n_idx = jnp.arange(N, dtype=jnp.float32)
    return {
        'x_prompt': nrm((BATCH, SEQ, D_MODEL)),
        'x_sample': nrm((DEC_BATCH, DEC_SEQ, D_MODEL)),
        'state_s5_re': nrm((L, DEC_BATCH, G, N), 0.1),
        'state_s5_im': nrm((L, DEC_BATCH, G, N), 0.1),
        'state_mlstm_c': nrm((L, DEC_BATCH, H, DH, DH), 0.1),
        'state_mlstm_n': nrm((L, DEC_BATCH, H, DH), 0.5),
        'state_mlstm_m': nrm((L, DEC_BATCH, H), 1.0),
        'state_conv': nrm((L, DEC_BATCH, CONV_WIDTH - 1, 2 * MLSTM_WIDTH)),
        'p_prompt': nrm((L, BATCH, SEQ, D_PLE)),
        'p_sample': nrm((L, DEC_BATCH, DEC_SEQ, D_PLE)),
        'g_ffn1': gain((L, D_MODEL)),
        'w1_gate': nrm((L, D_MODEL, D_FF), D_MODEL ** -0.5),
        'w1_up': nrm((L, D_MODEL, D_FF), D_MODEL ** -0.5),
        'w1_down': nrm((L, D_FF, D_MODEL), D_FF ** -0.5),
        'g_mix': gain((L, D_MODEL)),
        'w_in': nrm((L, D_MODEL, IN_COLS), D_MODEL ** -0.5),
        's5_lambda_re': -0.5 + nrm((L, G, N), 0.01),
        's5_lambda_im': math.pi * n_idx + nrm((L, G, N), 0.01),
        's5_log_dt': jax.random.uniform(nk(), (L, G), jnp.float32, math.log(1e-3), math.log(1e-1)),
        's5_b_re': nrm((L, G, N, S5_GROUP), S5_GROUP ** -0.5),
        's5_b_im': nrm((L, G, N, S5_GROUP), S5_GROUP ** -0.5),
        's5_c_re': nrm((L, G, S5_GROUP, N), N ** -0.5),
        's5_c_im': nrm((L, G, S5_GROUP, N), N ** -0.5),
        's5_d': nrm((L, S5_WIDTH)),
        's5_w_glu': nrm((L, S5_WIDTH, S5_WIDTH), S5_WIDTH ** -0.5),
        'w_s5_up': nrm((L, S5_WIDTH, D_MODEL), S5_WIDTH ** -0.5),
        'conv_w': nrm((L, CONV_WIDTH, 2 * MLSTM_WIDTH), CONV_WIDTH ** -0.5),
        'conv_b': nrm((L, 2 * MLSTM_WIDTH), 0.02),
        'b_igate': nrm((L, H), 0.1),
        'b_fgate': jax.random.uniform(nk(), (L, H), jnp.float32, 3.0, 6.0),
        'g_mhead': gain((L, MLSTM_WIDTH)),
        'w_m_up': nrm((L, MLSTM_WIDTH, D_MODEL), MLSTM_WIDTH ** -0.5),
        'w_out': nrm((L, D_MODEL, D_MODEL), D_MODEL ** -0.5),
        'g_ffn2': gain((L, D_MODEL)),
        'w2_gate': nrm((L, D_MODEL, D_FF), D_MODEL ** -0.5),
        'w2_up': nrm((L, D_MODEL, D_FF), D_MODEL ** -0.5),
        'w2_down': nrm((L, D_FF, D_MODEL), D_FF ** -0.5),
        'g_ple': gain((L, D_MODEL)),
        'w_ple': nrm((L, D_PLE, D_MODEL), D_PLE ** -0.5),
        'w_ple_gate': nrm((L, D_MODEL, D_MODEL), D_MODEL ** -0.5),
        'g_final': gain((D_MODEL,)),
    }


def reference(x_prompt, x_sample, state_s5_re, state_s5_im, state_mlstm_c, state_mlstm_n, state_mlstm_m,
              state_conv, p_prompt, p_sample, g_ffn1, w1_gate, w1_up, w1_down, g_mix, w_in,
              s5_lambda_re, s5_lambda_im, s5_log_dt, s5_b_re, s5_b_im, s5_c_re, s5_c_im, s5_d, s5_w_glu,
              w_s5_up, conv_w, conv_b, b_igate, b_fgate, g_mhead, w_m_up, w_out, g_ffn2, w2_gate, w2_up,
              w2_down, g_ple, w_ple, w_ple_gate, g_final):
    f32 = jnp.float32
    Bp = x_prompt.shape[0]
    H, DH = MLSTM_HEADS, MLSTM_HEAD_DIM
    z_s5 = jnp.zeros((Bp, S5_GROUPS, S5_STATE), f32)
    z_c = jnp.zeros((Bp, H, DH, DH), f32)
    z_n = jnp.zeros((Bp, H, DH), f32)
    z_m = jnp.zeros((Bp, H), f32)
    z_conv = jnp.zeros((Bp, CONV_WIDTH - 1, 2 * MLSTM_WIDTH), x_prompt.dtype)
    outs_p = [[] for _ in range(6)]
    outs_s = [[] for _ in range(6)]
    xp, xs = x_prompt, x_sample
    for i in range(DEPTH):
        lw = {
            'g_ffn1': g_ffn1[i], 'w1_gate': w1_gate[i], 'w1_up': w1_up[i], 'w1_down': w1_down[i],
            'g_mix': g_mix[i], 'w_in': w_in[i],
            's5_lambda_re': s5_lambda_re[i], 's5_lambda_im': s5_lambda_im[i], 's5_log_dt': s5_log_dt[i],
            's5_b_re': s5_b_re[i], 's5_b_im': s5_b_im[i], 's5_c_re': s5_c_re[i], 's5_c_im': s5_c_im[i],
            's5_d': s5_d[i], 's5_w_glu': s5_w_glu[i], 'w_s5_up': w_s5_up[i],
            'conv_w': conv_w[i], 'conv_b': conv_b[i], 'b_igate': b_igate[i], 'b_fgate': b_fgate[i],
            'g_mhead': g_mhead[i], 'w_m_up': w_m_up[i], 'w_out': w_out[i],
            'g_ffn2': g_ffn2[i], 'w2_gate': w2_gate[i], 'w2_up': w2_up[i], 'w2_down': w2_down[i],
            'g_ple': g_ple[i], 'w_ple': w_ple[i], 'w_ple_gate': w_ple_gate[i],
        }
        xp, st_p = _layer(xp, p_prompt[i], z_s5, z_s5, z_c, z_n, z_m, z_conv, lw)
        xs, st_s = _layer(xs, p_sample[i], state_s5_re[i], state_s5_im[i], state_mlstm_c[i],
                          state_mlstm_n[i], state_mlstm_m[i], state_conv[i], lw)
        for j in range(6):
            outs_p[j].append(st_p[j])
            outs_s[j].append(st_s[j])
    y_prompt = _rmsnorm(xp, g_final)
    y_sample = _rmsnorm(xs, g_final)
    return (y_prompt, y_sample,
            jnp.stack(outs_p[0]), jnp.stack(outs_p[1]), jnp.stack(outs_p[2]),
            jnp.stack(outs_p[3]), jnp.stack(outs_p[4]), jnp.stack(outs_p[5]),
            jnp.stack(outs_s[0]), jnp.stack(outs_s[1]), jnp.stack(outs_s[2]),
            jnp.stack(outs_s[3]), jnp.stack(outs_s[4]), jnp.stack(outs_s[5]))
```

```python
import functools
import math

import jax
import jax.numpy as jnp
from jax import lax
from jax.experimental import pallas as pl
from jax.experimental.pallas import tpu as pltpu

D_MODEL = 1024
D_PLE = 256
D_FF = 2048
S5_WIDTH = 512
S5_GROUP = 16
S5_GROUPS = S5_WIDTH // S5_GROUP
S5_STATE = 64
S5_LANES = S5_GROUPS * S5_STATE
MLSTM_WIDTH = 1024
MLSTM_HEADS = 4
MLSTM_HEAD_DIM = MLSTM_WIDTH // MLSTM_HEADS
MLSTM_CHUNK = 128
CONV_WIDTH = 4
EPS = 1e-6

F32 = jnp.float32
BF16 = jnp.bfloat16

SUBLANES = 8
LANES = 128
VMEM_LIMIT_BYTES = 56 * 1024 * 1024

ROW_TILE = 512
S5_TIME_CHUNK = 64
S5_BLOCKS = 4
S5_BLOCK_IN = S5_WIDTH // S5_BLOCKS
S5_BLOCK_ST = S5_LANES // S5_BLOCKS
S5_SCAN_LANES = 1024


def _params(sem):
    return pltpu.CompilerParams(dimension_semantics=sem, vmem_limit_bytes=VMEM_LIMIT_BYTES)


def _const_spec(shape):
    nd = len(shape)
    return pl.BlockSpec(shape, lambda *_: (0,) * nd, pipeline_mode=pl.Buffered(1))


def _rms(x, g):
    return x * lax.rsqrt(jnp.mean(x * x, axis=-1, keepdims=True) + EPS) * g


def _dot(a, b):
    return jnp.dot(a, b, preferred_element_type=F32)


def _ffn_body(x_ref, g_ref, wg_ref, wu_ref, wd_ref, o_ref):
    x = x_ref[...]
    h = _rms(x, g_ref[...]).astype(BF16)
    z = (jax.nn.silu(_dot(h, wg_ref[...])) * _dot(h, wu_ref[...])).astype(BF16)
    o_ref[...] = x + 0.5 * _dot(z, wd_ref[...])


def _ffn_ple_body(x_ref, p_ref, g_ref, wg_ref, wu_ref, wd_ref, gp_ref, wp_ref, wpg_ref, gf_ref, o_ref,
                  *, final):
    x = x_ref[...]
    h = _rms(x, g_ref[...]).astype(BF16)
    z = (jax.nn.silu(_dot(h, wg_ref[...])) * _dot(h, wu_ref[...])).astype(BF16)
    x = x + 0.5 * _dot(z, wd_ref[...])
    gate = jax.nn.sigmoid(_dot(_rms(x, gp_ref[...]).astype(BF16), wpg_ref[...]))
    x = x + _dot(p_ref[...].astype(BF16), wp_ref[...]) * gate
    if final:
        x = _rms(x, gf_ref[...])
    o_ref[...] = x


def _ffn(x, g, wg, wu, wd):
    rows = x.shape[0]
    tm = min(ROW_TILE, rows)
    row_spec = pl.BlockSpec((tm, D_MODEL), lambda i: (i, 0))
    return pl.pallas_call(
        _ffn_body,
        grid=(rows // tm,),
        in_specs=[row_spec, _const_spec((1, D_MODEL)), _const_spec((D_MODEL, D_FF)),
                  _const_spec((D_MODEL, D_FF)), _const_spec((D_FF, D_MODEL))],
        out_specs=row_spec,
        out_shape=jax.ShapeDtypeStruct((rows, D_MODEL), F32),
        compiler_params=_params(("parallel",)),
        name="ffn",
    )(x, g, wg, wu, wd)


def _ffn_ple(x, p, g, wg, wu, wd, gp, wp, wpg, gf, final):
    rows = x.shape[0]
    tm = min(ROW_TILE, rows)
    row_spec = pl.BlockSpec((tm, D_MODEL), lambda i: (i, 0))
    return pl.pallas_call(
        functools.partial(_ffn_ple_body, final=final),
        grid=(rows // tm,),
        in_specs=[row_spec, pl.BlockSpec((tm, D_PLE), lambda i: (i, 0)),
                  _const_spec((1, D_MODEL)), _const_spec((D_MODEL, D_FF)),
                  _const_spec((D_MODEL, D_FF)), _const_spec((D_FF, D_MODEL)),
                  _const_spec((1, D_MODEL)), _const_spec((D_PLE, D_MODEL)),
                  _const_spec((D_MODEL, D_MODEL)), _const_spec((1, D_MODEL))],
        out_specs=row_spec,
        out_shape=jax.ShapeDtypeStruct((rows, D_MODEL), F32),
        compiler_params=_params(("parallel",)),
        name="ffn_ple",
    )(x, p, g, wg, wu, wd, gp, wp, wpg, gf)


def _inproj_body(x_ref, g_ref, wu_ref, wqk_ref, wv_ref, wgc_ref, wgr_ref,
                 u_ref, qk_ref, v_ref, gc_ref, gr_ref):
    h = _rms(x_ref[...], g_ref[...]).astype(BF16)
    u_ref[...] = _dot(h, wu_ref[...])
    qk_ref[...] = _dot(h, wqk_ref[...])
    v_ref[...] = _dot(h, wv_ref[...])
    gc_ref[...] = _dot(h, wgc_ref[...])
    gr = lax.dot_general(wgr_ref[...], h, (((1,), (1,)), ((), ())), preferred_element_type=F32)
    for k in range(gr_ref.shape[0]):
        gr_ref[k] = gr[:, k * LANES:(k + 1) * LANES]


def _inproj(x, g, wu, wqk, wv, wgc, wgr, n_batch):
    rows = x.shape[0]
    seq = rows // n_batch
    tm = min(ROW_TILE, seq)
    nt = seq // tm
    rb = lambda b, t: (b * nt + t, 0)
    in_specs = [pl.BlockSpec((tm, D_MODEL), rb), _const_spec((1, D_MODEL)),
                _const_spec((D_MODEL, S5_WIDTH)), _const_spec((D_MODEL, 2 * MLSTM_WIDTH)),
                _const_spec((D_MODEL, MLSTM_WIDTH)), _const_spec((D_MODEL, LANES)),
                _const_spec((SUBLANES, D_MODEL))]
    out_specs = [pl.BlockSpec((tm, S5_WIDTH), lambda b, t: (t, b)),
                 pl.BlockSpec((tm, 2 * MLSTM_WIDTH), rb),
                 pl.BlockSpec((tm, MLSTM_WIDTH), rb),
                 pl.BlockSpec((tm, LANES), rb),
                 pl.BlockSpec((tm // LANES, SUBLANES, LANES), lambda b, t: (b * nt + t, 0, 0))]
    out_shape = [jax.ShapeDtypeStruct((seq, n_batch * S5_WIDTH), F32),
                 jax.ShapeDtypeStruct((rows, 2 * MLSTM_WIDTH), F32),
                 jax.ShapeDtypeStruct((rows, MLSTM_WIDTH), F32),
                 jax.ShapeDtypeStruct((rows, LANES), F32),
                 jax.ShapeDtypeStruct((rows // LANES, SUBLANES, LANES), F32)]
    return pl.pallas_call(
        _inproj_body,
        grid=(n_batch, nt),
        in_specs=in_specs,
        out_specs=out_specs,
        out_shape=out_shape,
        compiler_params=_params(("parallel", "parallel")),
        name="inproj",
    )(x, g, wu, wqk, wv, wgc, wgr)


def _s5_param_body(lre_ref, lim_ref, ldt_ref, bre_ref, bim_ref, are_ref, aim_ref, bbre_ref, bbim_ref):
    lre = lre_ref[0]
    lim = lim_ref[0]
    dt = jnp.exp(ldt_ref[0])
    mag = jnp.exp(lre * dt)
    a_re = mag * jnp.cos(lim * dt)
    a_im = mag * jnp.sin(lim * dt)
    den = lre * lre + lim * lim
    pr = a_re - 1.0
    w_re = (pr * lre + a_im * lim) / den
    w_im = (a_im * lre - pr * lim) / den
    are_ref[0] = a_re
    aim_ref[0] = a_im
    bbre_ref[0] = w_re * bre_ref[0] - w_im * bim_ref[0]
    bbim_ref[0] = w_re * bim_ref[0] + w_im * bre_ref[0]


def _s5_params(lre, lim, ldt, bre, bim):
    depth = lre.shape[0]
    vec = pl.BlockSpec((1, 1, S5_LANES), lambda i: (i, 0, 0))
    mat = pl.BlockSpec((1, S5_GROUP, S5_LANES), lambda i: (i, 0, 0))
    return pl.pallas_call(
        _s5_param_body,
        grid=(depth,),
        in_specs=[vec, vec, vec, mat, mat],
        out_specs=[vec, vec, mat, mat],
        out_shape=[jax.ShapeDtypeStruct((depth, 1, S5_LANES), F32)] * 2
        + [jax.ShapeDtypeStruct((depth, S5_GROUP, S5_LANES), F32)] * 2,
        compiler_params=_params(("parallel",)),
        name="s5_params",
    )(lre, lim, ldt, bre, bim)


def _s5_body(u_ref, are_ref, aim_ref, bm_ref, cre_ref, cim_ref, d_ref, h0r_ref, h0i_ref,
             y_ref, sr_ref, si_ref, hr_s, hi_s, *, n_batch, n_time):
    @pl.when(pl.program_id(0) == 0)
    def _():
        sr_ref[...] = h0r_ref[...]
        si_ref[...] = h0i_ref[...]

    u = u_ref[...]
    ub = u.astype(BF16)
    for j in range(S5_BLOCKS):
        r = _dot(ub[:, j * S5_BLOCK_IN:(j + 1) * S5_BLOCK_IN], bm_ref[j])
        hr_s[:, j * S5_BLOCK_ST:(j + 1) * S5_BLOCK_ST] = r[:, :S5_BLOCK_ST]
        hi_s[:, j * S5_BLOCK_ST:(j + 1) * S5_BLOCK_ST] = r[:, S5_BLOCK_ST:]

    for bt in range(n_batch // SUBLANES):
        for q in range(S5_LANES // S5_SCAN_LANES):
            lanes = pl.ds(q * S5_SCAN_LANES, S5_SCAN_LANES)
            brow = pl.ds(bt * SUBLANES, SUBLANES)
            a_re = jnp.broadcast_to(are_ref[:, lanes], (SUBLANES, S5_SCAN_LANES))
            a_im = jnp.broadcast_to(aim_ref[:, lanes], (SUBLANES, S5_SCAN_LANES))

            def step(t, carry, lanes=lanes, a_re=a_re, a_im=a_im, bt=bt):
                h_re, h_im = carry
                rows = pl.ds(pl.multiple_of(t * n_batch + bt * SUBLANES, SUBLANES), SUBLANES)
                n_re = a_re * h_re - a_im * h_im + hr_s[rows, lanes]
                n_im = a_re * h_im + a_im * h_re + hi_s[rows, lanes]
                hr_s[rows, lanes] = n_re
                hi_s[rows, lanes] = n_im
                return n_re, n_im

            h_re, h_im = lax.fori_loop(0, n_time, step, (sr_ref[brow, lanes], si_ref[brow, lanes]),
                                       unroll=2)
            sr_ref[brow, lanes] = h_re
            si_ref[brow, lanes] = h_im

    d = d_ref[...]
    for j in range(S5_BLOCKS):
        st = slice(j * S5_BLOCK_ST, (j + 1) * S5_BLOCK_ST)
        ch = slice(j * S5_BLOCK_IN, (j + 1) * S5_BLOCK_IN)
        y = _dot(hr_s[:, st].astype(BF16), cre_ref[j]) - _dot(hi_s[:, st].astype(BF16), cim_ref[j])
        y_ref[:, ch] = y + d[:, ch] * u[:, ch]


def _s5(u, a_re, a_im, bm, cre, cim, d, h0r, h0i, n_batch, n_time):
    rows = u.shape[0]
    blk = n_time * n_batch
    return pl.pallas_call(
        functools.partial(_s5_body, n_batch=n_batch, n_time=n_time),
        grid=(rows // blk,),
        in_specs=[pl.BlockSpec((blk, S5_WIDTH), lambda c: (c, 0)),
                  _const_spec((1, S5_LANES)), _const_spec((1, S5_LANES)),
                  _const_spec((S5_BLOCKS, S5_BLOCK_IN, 2 * S5_BLOCK_ST)),
                  _const_spec((S5_BLOCKS, S5_BLOCK_ST, S5_BLOCK_IN)),
                  _const_spec((S5_BLOCKS, S5_BLOCK_ST, S5_BLOCK_IN)),
                  _const_spec((1, S5_WIDTH)),
                  _const_spec((n_batch, S5_LANES)), _const_spec((n_batch, S5_LANES))],
        out_specs=[pl.BlockSpec((blk, S5_WIDTH), lambda c: (c, 0)),
                   pl.BlockSpec((n_batch, S5_LANES), lambda c: (0, 0)),
                   pl.BlockSpec((n_batch, S5_LANES), lambda c: (0, 0))],
        out_shape=[jax.ShapeDtypeStruct((rows, S5_WIDTH), F32),
                   jax.ShapeDtypeStruct((n_batch, S5_LANES), F32),
                   jax.ShapeDtypeStruct((n_batch, S5_LANES), F32)],
        scratch_shapes=[pltpu.VMEM((blk, S5_LANES), F32), pltpu.VMEM((blk, S5_LANES), F32)],
        compiler_params=_params(("arbitrary",)),
        name="s5",
    )(u, a_re, a_im, bm, cre, cim, d, h0r, h0i)


def _mlstm_body(qk_ref, v_ref, gc_ref, gr_ref, cw_ref, cb_ref, bc_ref, br_ref, gm_ref,
                c0_ref, n0_ref, m0_ref, conv0_ref,
                hm_ref, c_ref, n_ref, m_ref, conv_ref, xp_s, *, chunk, valid):
    hist = SUBLANES

    @pl.when(pl.program_id(1) == 0)
    def _():
        c_ref[...] = c0_ref[...]
        n_ref[...] = n0_ref[...]
        m_ref[...] = m0_ref[...]
        xp_s[hist - (CONV_WIDTH - 1):hist, :] = conv0_ref[0]

    x_raw = qk_ref[...]
    xp_s[hist:hist + chunk, :] = x_raw
    qk = jnp.broadcast_to(cb_ref[...], x_raw.shape)
    for j in range(CONV_WIDTH):
        off = hist - (CONV_WIDTH - 1) + j
        qk = qk + xp_s[off:off + chunk, :] * cw_ref[j:j + 1, :]
    new_hist = xp_s[hist + valid - (CONV_WIDTH - 1):hist + valid, :]
    xp_s[hist - (CONV_WIDTH - 1):hist, :] = new_hist
    conv_ref[0] = new_hist
    qk = jax.nn.silu(qk)
    q_all = qk[:, :MLSTM_WIDTH] * (MLSTM_HEAD_DIM ** -0.5)
    k_all = qk[:, MLSTM_WIDTH:]
    v_all = v_ref[...]

    row_id = lax.broadcasted_iota(jnp.int32, (chunk, 1), 0)
    col_id = lax.broadcasted_iota(jnp.int32, (1, chunk), 1)
    t_id = lax.broadcasted_iota(jnp.int32, (chunk, chunk), 0)
    s_id = lax.broadcasted_iota(jnp.int32, (chunk, chunk), 1)
    causal = s_id <= t_id
    tril = causal.astype(F32)
    triu = (t_id <= s_id).astype(F32)

    gcv = gc_ref[...] + bc_ref[...]
    grv = gr_ref[0] + br_ref[...]
    li_col = jnp.where(row_id < valid, gcv, -jnp.inf)
    li_row = jnp.where(col_id < valid, grv, -jnp.inf)
    lf_col = jnp.where(row_id < valid, jax.nn.log_sigmoid(gcv), 0.0)
    lf_row = jnp.where(col_id < valid, jax.nn.log_sigmoid(grv), 0.0)
    b_col = jnp.dot(tril, lf_col, preferred_element_type=F32, precision=lax.Precision.HIGHEST)
    b_row = jnp.dot(lf_row, triu, preferred_element_type=F32, precision=lax.Precision.HIGHEST)

    m_all = m_ref[0]
    m_new_parts = []
    for h in range(MLSTM_HEADS):
        hs = slice(h * MLSTM_HEAD_DIM, (h + 1) * MLSTM_HEAD_DIM)
        bc = b_col[:, MLSTM_HEADS + h:MLSTM_HEADS + h + 1]
        br = b_row[MLSTM_HEADS + h:MLSTM_HEADS + h + 1, :]
        lic = li_col[:, h:h + 1]
        lir = li_row[h:h + 1, :]
        m_prev = m_all[:, h:h + 1]
        qf = q_all[:, hs]
        kf = k_all[:, hs]
        qb = qf.astype(BF16)
        vb = v_all[:, hs].astype(BF16)
        c_old = c_ref[0, h]
        n_old = n_ref[0, h:h + 1, :]

        dmat = jnp.where(causal, bc - br + lir, -jnp.inf)
        inter = bc + m_prev
        m_t = jnp.maximum(inter, jnp.max(dmat, axis=-1, keepdims=True))
        w = jnp.exp(dmat - m_t)
        scores = lax.dot_general(qb, kf.astype(BF16), (((1,), (1,)), ((), ())),
                                 preferred_element_type=F32) * w
        a = jnp.exp(inter - m_t)
        num = _dot(scores.astype(BF16), vb) + a * _dot(qb, c_old.astype(BF16))
        nq = (jnp.sum(scores, axis=-1, keepdims=True)
              + a * jnp.sum(qf * n_old, axis=-1, keepdims=True))
        hout = num / jnp.maximum(jnp.abs(nq), jnp.exp(-m_t))
        hout = hout * lax.rsqrt(jnp.mean(hout * hout, axis=-1, keepdims=True) + EPS)
        hm_ref[:, hs] = hout * gm_ref[:, hs]

        b_last = br[:, chunk - 1:chunk]
        g_row = b_last - br + lir
        m_new = jnp.maximum(b_last + m_prev, jnp.max(g_row, axis=-1, keepdims=True))
        wk = jnp.exp(b_last - bc + lic - m_new)
        decay = jnp.exp(b_last + m_prev - m_new)
        kw = kf * wk
        c_ref[0, h] = decay * c_old + lax.dot_general(
            kw.astype(BF16), vb, (((0,), (0,)), ((), ())), preferred_element_type=F32)
        n_ref[0, h:h + 1, :] = decay * n_old + jnp.sum(kw, axis=0, keepdims=True)
        m_new_parts.append(m_new)
    m_ref[0] = jnp.concatenate(m_new_parts, axis=1)


def _mlstm(qk, v, gc, gr, cw, cb, bias_c, bias_r, gm, c0, n0, m0, conv0, n_seq, n_chunk, chunk, valid):
    rows = qk.shape[0]
    hh, dh = MLSTM_HEADS, MLSTM_HEAD_DIM
    rb = lambda b, c: (b * n_chunk + c, 0)
    seq4 = lambda b, c: (b, 0, 0, 0)
    seq3 = lambda b, c: (b, 0, 0)
    in_specs = [pl.BlockSpec((chunk, 2 * MLSTM_WIDTH), rb),
                pl.BlockSpec((chunk, MLSTM_WIDTH), rb),
                pl.BlockSpec((chunk, LANES), rb),
                pl.BlockSpec((1, SUBLANES, chunk), lambda b, c: (b * n_chunk + c, 0, 0)),
                _const_spec((CONV_WIDTH, 2 * MLSTM_WIDTH)), _const_spec((1, 2 * MLSTM_WIDTH)),
                _const_spec((1, LANES)), _const_spec((SUBLANES, 1)), _const_spec((1, MLSTM_WIDTH)),
                pl.BlockSpec((1, hh, dh, dh), seq4), pl.BlockSpec((1, hh, dh), seq3),
                pl.BlockSpec((1, 1, hh), seq3), pl.BlockSpec((1, CONV_WIDTH - 1, 2 * MLSTM_WIDTH), seq3)]
    out_specs = [pl.BlockSpec((chunk, MLSTM_WIDTH), rb),
                 pl.BlockSpec((1, hh, dh, dh), seq4), pl.BlockSpec((1, hh, dh), seq3),
                 pl.BlockSpec((1, 1, hh), seq3), pl.BlockSpec((1, CONV_WIDTH - 1, 2 * MLSTM_WIDTH), seq3)]
    out_shape = [jax.ShapeDtypeStruct((rows, MLSTM_WIDTH), F32),
                 jax.ShapeDtypeStruct((n_seq, hh, dh, dh), F32),
                 jax.ShapeDtypeStruct((n_seq, hh, dh), F32),
                 jax.ShapeDtypeStruct((n_seq, 1, hh), F32),
                 jax.ShapeDtypeStruct((n_seq, CONV_WIDTH - 1, 2 * MLSTM_WIDTH), F32)]
    return pl.pallas_call(
        functools.partial(_mlstm_body, chunk=chunk, valid=valid),
        grid=(n_seq, n_chunk),
        in_specs=in_specs,
        out_specs=out_specs,
        out_shape=out_shape,
        scratch_shapes=[pltpu.VMEM((SUBLANES + chunk, 2 * MLSTM_WIDTH), F32)],
        compiler_params=_params(("parallel", "arbitrary")),
        name="mlstm",
    )(qk, v, gc, gr, cw, cb, bias_c, bias_r, gm, c0, n0, m0, conv0)


def _post_body(x_ref, y_ref, hm_ref, g_ref, wo_ref, wgs_ref, wgm_ref, wglu_ref, wsu_ref, wmu_ref,
               wout_ref, o_ref):
    x = x_ref[...]
    h = _rms(x, g_ref[...]).astype(BF16)
    ys = jax.nn.gelu(y_ref[...])
    ys = ys * jax.nn.sigmoid(_dot(ys.astype(BF16), wglu_ref[...]))
    hm = hm_ref[...] * jax.nn.sigmoid(_dot(h, wo_ref[...]))
    merged = (jax.nn.sigmoid(_dot(h, wgs_ref[...])) * _dot(ys.astype(BF16), wsu_ref[...])
              + jax.nn.sigmoid(_dot(h, wgm_ref[...])) * _dot(hm.astype(BF16), wmu_ref[...]))
    o_ref[...] = x + _dot(merged.astype(BF16), wout_ref[...])


def _post(x, y, hm, g, wo, wgs, wgm, wglu, wsu, wmu, wout, n_batch):
    rows = x.shape[0]
    seq = rows // n_batch
    tm = min(ROW_TILE, seq)
    nt = seq // tm
    rb = lambda b, t: (b * nt + t, 0)
    sq = _const_spec((D_MODEL, D_MODEL))
    return pl.pallas_call(
        _post_body,
        grid=(n_batch, nt),
        in_specs=[pl.BlockSpec((tm, D_MODEL), rb),
                  pl.BlockSpec((tm, S5_WIDTH), lambda b, t: (t, b)),
                  pl.BlockSpec((tm, MLSTM_WIDTH), rb),
                  _const_spec((1, D_MODEL)), sq, sq, sq,
                  _const_spec((S5_WIDTH, S5_WIDTH)), _const_spec((S5_WIDTH, D_MODEL)), sq, sq],
        out_specs=pl.BlockSpec((tm, D_MODEL), rb),
        out_shape=jax.ShapeDtypeStruct((rows, D_MODEL), F32),
        compiler_params=_params(("parallel", "parallel")),
        name="post",
    )(x, y, hm, g, wo, wgs, wgm, wglu, wsu, wmu, wout)


def _block_diag(t):
    depth, nb, gpb, a, c = t.shape
    eye = jnp.eye(gpb, dtype=t.dtype)
    return jnp.einsum("ljgac,gk->ljgakc", t, eye).reshape(depth, nb, gpb * a, gpb * c)


def kernel(x_prompt, x_sample, state_s5_re, state_s5_im, state_mlstm_c, state_mlstm_n, state_mlstm_m, state_conv, p_prompt, p_sample, g_ffn1, w1_gate, w1_up, w1_down, g_mix, w_in, s5_lambda_re, s5_lambda_im, s5_log_dt, s5_b_re, s5_b_im, s5_c_re, s5_c_im, s5_d, s5_w_glu, w_s5_up, conv_w, conv_b, b_igate, b_fgate, g_mhead, w_m_up, w_out, g_ffn2, w2_gate, w2_up, w2_down, g_ple, w_ple, w_ple_gate, g_final):
    depth = w_in.shape[0]
    bp, sp, _ = x_prompt.shape
    bs, ss, _ = x_sample.shape
    hh, dh = MLSTM_HEADS, MLSTM_HEAD_DIM
    gpb = S5_GROUPS // S5_BLOCKS
    pad_t = SUBLANES

    bf = lambda w: w.astype(BF16)
    cuts = [0, S5_WIDTH, S5_WIDTH + 2 * MLSTM_WIDTH, S5_WIDTH + 3 * MLSTM_WIDTH, S5_WIDTH + 4 * MLSTM_WIDTH]
    c_gate = cuts[4] + 2 * MLSTM_HEADS
    w_u = bf(w_in[:, :, cuts[0]:cuts[1]])
    w_qk = bf(w_in[:, :, cuts[1]:cuts[2]])
    w_v = bf(w_in[:, :, cuts[2]:cuts[3]])
    w_o = bf(w_in[:, :, cuts[3]:cuts[4]])
    w_if = w_in[:, :, cuts[4]:c_gate]
    w_gc = bf(jnp.pad(w_if, ((0, 0), (0, 0), (0, LANES - 2 * MLSTM_HEADS))))
    w_gr = bf(jnp.swapaxes(w_if, 1, 2))
    w_gs = bf(w_in[:, :, c_gate:c_gate + D_MODEL])
    w_gm = bf(w_in[:, :, c_gate + D_MODEL:])
    w1g, w1u, w1d = bf(w1_gate), bf(w1_up), bf(w1_down)
    w2g, w2u, w2d = bf(w2_gate), bf(w2_up), bf(w2_down)
    wglu, wsu, wmu, wout = bf(s5_w_glu), bf(w_s5_up), bf(w_m_up), bf(w_out)
    wple, wpg = bf(w_ple), bf(w_ple_gate)
    row = lambda g: g.reshape(depth, 1, -1)
    gf1, gmx, gf2, gpl, gmh, s5d, cvb = (row(g_ffn1), row(g_mix), row(g_ffn2), row(g_ple),
                                          row(g_mhead), row(s5_d), row(conv_b))
    gfin = g_final.reshape(1, D_MODEL)
    bias = jnp.concatenate([b_igate, b_fgate], axis=1)
    bias_c = jnp.pad(bias, ((0, 0), (0, LANES - 2 * MLSTM_HEADS))).reshape(depth, 1, LANES)
    bias_r = bias.reshape(depth, SUBLANES, 1)

    lanes3 = lambda t: t.reshape(depth, 1, S5_LANES)
    ldt = jnp.broadcast_to(s5_log_dt[:, :, None], (depth, S5_GROUPS, S5_STATE))
    to_cols = lambda t: jnp.transpose(t, (0, 3, 1, 2)).reshape(depth, S5_GROUP, S5_LANES)
    a_re, a_im, bb_re, bb_im = _s5_params(lanes3(s5_lambda_re), lanes3(s5_lambda_im), lanes3(ldt),
                                          to_cols(s5_b_re), to_cols(s5_b_im))
    to_blk = lambda t: jnp.transpose(t.reshape(depth, S5_GROUP, S5_BLOCKS, gpb, S5_STATE), (0, 2, 3, 1, 4))
    bmat = bf(jnp.concatenate([_block_diag(to_blk(bb_re)), _block_diag(to_blk(bb_im))], axis=-1))
    c_blk = lambda t: jnp.transpose(t.reshape(depth, S5_BLOCKS, gpb, S5_GROUP, S5_STATE), (0, 1, 2, 4, 3))
    cmat_re = bf(_block_diag(c_blk(s5_c_re)))
    cmat_im = bf(_block_diag(c_blk(s5_c_im)))

    xp = x_prompt.reshape(bp * sp, D_MODEL)
    xs = x_sample.reshape(bs * ss, D_MODEL)
    pp = p_prompt.reshape(depth, bp * sp, D_PLE)
    ps = p_sample.reshape(depth, bs * ss, D_PLE)
    n_chunk = sp // MLSTM_CHUNK
    zs5 = jnp.zeros((bp, S5_LANES), F32)
    zc = jnp.zeros((bp, hh, dh, dh), F32)
    zn = jnp.zeros((bp, hh, dh), F32)
    zm = jnp.zeros((bp, 1, hh), F32)
    zconv = jnp.zeros((bp, CONV_WIDTH - 1, 2 * MLSTM_WIDTH), F32)

    def pad_seq(t):
        t = t.reshape(bs, ss, -1)
        return jnp.pad(t, ((0, 0), (0, pad_t - ss), (0, 0))).reshape(bs * pad_t, -1)

    outs_p = [[] for _ in range(6)]
    outs_s = [[] for _ in range(6)]
    for i in range(depth):
        final = i == depth - 1
        mix_w = (gmx[i], w_o[i], w_gs[i], w_gm[i], wglu[i], wsu[i], wmu[i], wout[i])
        ple_w = (gf2[i], w2g[i], w2u[i], w2d[i], gpl[i], wple[i], wpg[i], gfin)
        lstm_w = (conv_w[i], cvb[i], bias_c[i], bias_r[i], gmh[i])
        s5_w = (a_re[i], a_im[i], bmat[i], cmat_re[i], cmat_im[i], s5d[i])
        proj_w = (gmx[i], w_u[i], w_qk[i], w_v[i], w_gc[i], w_gr[i])

        xp = _ffn(xp, gf1[i], w1g[i], w1u[i], w1d[i])
        u, qk, v, gc, gr = _inproj(xp, *proj_w, n_batch=bp)
        y, s5r, s5i = _s5(u.reshape(sp * bp, S5_WIDTH), *s5_w, zs5, zs5,
                          n_batch=bp, n_time=S5_TIME_CHUNK)
        hm, c_n, n_n, m_n, conv_n = _mlstm(qk, v, gc, gr, *lstm_w, zc, zn, zm, zconv,
                                           n_seq=bp, n_chunk=n_chunk, chunk=MLSTM_CHUNK, valid=MLSTM_CHUNK)
        xp = _post(xp, y.reshape(sp, bp * S5_WIDTH), hm, *mix_w, n_batch=bp)
        xp = _ffn_ple(xp, pp[i], *ple_w, final=final)
        for j, t in enumerate((s5r.reshape(bp, S5_GROUPS, S5_STATE), s5i.reshape(bp, S5_GROUPS, S5_STATE),
                               c_n, n_n, m_n.reshape(bp, hh), conv_n)):
            outs_p[j].append(t)

        xs = _ffn(xs, gf1[i], w1g[i], w1u[i], w1d[i])
        u, qk, v, gc, gr = _inproj(xs, *proj_w, n_batch=1)
        u_tm = jnp.swapaxes(u.reshape(bs, ss, S5_WIDTH), 0, 1).reshape(ss * bs, S5_WIDTH)
        y_tm, s5r, s5i = _s5(u_tm, *s5_w, state_s5_re[i].reshape(bs, S5_LANES),
                             state_s5_im[i].reshape(bs, S5_LANES), n_batch=bs, n_time=ss)
        y = jnp.swapaxes(y_tm.reshape(ss, bs, S5_WIDTH), 0, 1).reshape(bs * ss, S5_WIDTH)
        gr_s = jnp.transpose(gr, (1, 0, 2)).reshape(SUBLANES, bs, ss)
        gr_s = jnp.transpose(jnp.pad(gr_s, ((0, 0), (0, 0), (0, pad_t - ss))), (1, 0, 2))
        hm, c_n, n_n, m_n, conv_n = _mlstm(pad_seq(qk), pad_seq(v), pad_seq(gc), gr_s, *lstm_w,
                                           state_mlstm_c[i], state_mlstm_n[i],
                                           state_mlstm_m[i].reshape(bs, 1, hh), state_conv[i],
                                           n_seq=bs, n_chunk=1, chunk=pad_t, valid=ss)
        hm = hm.reshape(bs, pad_t, MLSTM_WIDTH)[:, :ss].reshape(bs * ss, MLSTM_WIDTH)
        xs = _post(xs, y, hm, *mix_w, n_batch=1)
        xs = _ffn_ple(xs, ps[i], *ple_w, final=final)
        for j, t in enumerate((s5r.reshape(bs, S5_GROUPS, S5_STATE), s5i.reshape(bs, S5_GROUPS, S5_STATE),
                               c_n, n_n, m_n.reshape(bs, hh), conv_n)):
            outs_s[j].append(t)

    y_prompt = xp.reshape(bp, sp, D_MODEL)
    y_sample = xs.reshape(bs, ss, D_MODEL)
    return (y_prompt, y_sample,
            jnp.stack(outs_p[0]), jnp.stack(outs_p[1]), jnp.stack(outs_p[2]),
            jnp.stack(outs_p[3]), jnp.stack(outs_p[4]), jnp.stack(outs_p[5]),
            jnp.stack(outs_s[0]), jnp.stack(outs_s[1]), jnp.stack(outs_s[2]),
            jnp.stack(outs_s[3]), jnp.stack(outs_s[4]), jnp.stack(outs_s[5]))
```

```python
import functools

import jax
import jax.numpy as jnp
from jax import lax
from jax.experimental import pallas as pl
from jax.experimental.pallas import tpu as pltpu

D_MODEL = 1024
D_PLE = 256
D_FF = 2048
S5_WIDTH = 512
S5_GROUP = 16
S5_GROUPS = S5_WIDTH // S5_GROUP
S5_STATE = 64
S5_LANES = S5_GROUPS * S5_STATE
MLSTM_WIDTH = 1024
MLSTM_HEADS = 4
MLSTM_HEAD_DIM = MLSTM_WIDTH // MLSTM_HEADS
MLSTM_CHUNK = 128
CONV_WIDTH = 4
CONV_HIST = CONV_WIDTH - 1
EPS = 1e-6

F32 = jnp.float32
BF16 = jnp.bfloat16
HIGHEST = lax.Precision.HIGHEST

SUBLANES = 8
LANES = 128
VMEM_LIMIT_BYTES = 56 * 1024 * 1024

ROW_TILE = 512
S5_ROWS = 512
S5_BLOCKS = 4
S5_BLOCK_IN = S5_WIDTH // S5_BLOCKS
S5_BLOCK_ST = S5_LANES // S5_BLOCKS
S5_SCAN_LANES = 1024
SAMPLE_SEQS_PER_STEP = 2


def _params(sem):
    return pltpu.CompilerParams(dimension_semantics=sem, vmem_limit_bytes=VMEM_LIMIT_BYTES)


def _layer_spec(shape, layer):
    nd = len(shape)
    return pl.BlockSpec((None,) + tuple(shape), lambda *_: (layer,) + (0,) * nd,
                        pipeline_mode=pl.Buffered(1))


def _const_spec(shape):
    nd = len(shape)
    return pl.BlockSpec(tuple(shape), lambda *_: (0,) * nd, pipeline_mode=pl.Buffered(1))


_ANY = pl.BlockSpec(memory_space=pl.ANY)


def _rms(x, g):
    return x * lax.rsqrt(jnp.mean(x * x, axis=-1, keepdims=True) + EPS) * g


def _dot(a, b):
    return jnp.dot(a, b, preferred_element_type=F32)


def _dot_nt(a, b, **kw):
    return lax.dot_general(a, b, (((1,), (1,)), ((), ())), preferred_element_type=F32, **kw)


def _dot_tn(a, b):
    return lax.dot_general(a, b, (((0,), (0,)), ((), ())), preferred_element_type=F32)


def _ffn_body(x_ref, g_ref, wg_ref, wu_ref, wd_ref, o_ref):
    x = x_ref[...]
    h = _rms(x, g_ref[...]).astype(BF16)
    z = (jax.nn.silu(_dot(h, wg_ref[...])) * _dot(h, wu_ref[...])).astype(BF16)
    o_ref[...] = x + 0.5 * _dot(z, wd_ref[...])


def _ffn_ple_body(x_ref, p_ref, g_ref, wg_ref, wu_ref, wd_ref, gp_ref, wp_ref, wpg_ref, gf_ref, o_ref,
                  *, final):
    x = x_ref[...]
    h = _rms(x, g_ref[...]).astype(BF16)
    z = (jax.nn.silu(_dot(h, wg_ref[...])) * _dot(h, wu_ref[...])).astype(BF16)
    x = x + 0.5 * _dot(z, wd_ref[...])
    gate = jax.nn.sigmoid(_dot(_rms(x, gp_ref[...]).astype(BF16), wpg_ref[...]))
    x = x + _dot(p_ref[...].astype(BF16), wp_ref[...]) * gate
    if final:
        x = _rms(x, gf_ref[...])
    o_ref[...] = x


def _ffn(x, g, wg, wu, wd, layer):
    rows = x.shape[0]
    tm = min(ROW_TILE, rows)
    row_spec = pl.BlockSpec((tm, D_MODEL), lambda i: (i, 0))
    return pl.pallas_call(
        _ffn_body,
        grid=(rows // tm,),
        in_specs=[row_spec, _layer_spec((1, D_MODEL), layer), _layer_spec((D_MODEL, D_FF), layer),
                  _layer_spec((D_MODEL, D_FF), layer), _layer_spec((D_FF, D_MODEL), layer)],
        out_specs=row_spec,
        out_shape=jax.ShapeDtypeStruct((rows, D_MODEL), F32),
        compiler_params=_params(("parallel",)),
        name="ffn",
    )(x, g, wg, wu, wd)


def _ffn_ple(x, p, g, wg, wu, wd, gp, wp, wpg, gf, layer, final):
    rows = x.shape[0]
    tm = min(ROW_TILE, rows)
    row_spec = pl.BlockSpec((tm, D_MODEL), lambda i: (i, 0))
    return pl.pallas_call(
        functools.partial(_ffn_ple_body, final=final),
        grid=(rows // tm,),
        in_specs=[row_spec, pl.BlockSpec((None, tm, D_PLE), lambda i: (layer, i, 0)),
                  _layer_spec((1, D_MODEL), layer), _layer_spec((D_MODEL, D_FF), layer),
                  _layer_spec((D_MODEL, D_FF), layer), _layer_spec((D_FF, D_MODEL), layer),
                  _layer_spec((1, D_MODEL), layer), _layer_spec((D_PLE, D_MODEL), layer),
                  _layer_spec((D_MODEL, D_MODEL), layer), _const_spec((1, D_MODEL))],
        out_specs=row_spec,
        out_shape=jax.ShapeDtypeStruct((rows, D_MODEL), F32),
        compiler_params=_params(("parallel",)),
        name="ffn_ple",
    )(x, p, g, wg, wu, wd, gp, wp, wpg, gf)


def _inproj_body(x_ref, g_ref, wu_ref, wqk_ref, wv_ref, wgc_ref, u_ref, qk_ref, v_ref, gc_ref):
    h = _rms(x_ref[...], g_ref[...]).astype(BF16)
    u_ref[...] = _dot(h, wu_ref[...])
    qk_ref[...] = _dot(h, wqk_ref[...])
    v_ref[...] = _dot(h, wv_ref[...])
    gc_ref[...] = _dot(h, wgc_ref[...])


def _inproj(x, g, wu, wqk, wv, wgc, layer):
    rows = x.shape[0]
    tm = min(ROW_TILE, rows)
    rb = lambda i: (i, 0)
    widths = (S5_WIDTH, 2 * MLSTM_WIDTH, MLSTM_WIDTH, LANES)
    return pl.pallas_call(
        _inproj_body,
        grid=(rows // tm,),
        in_specs=[pl.BlockSpec((tm, D_MODEL), rb), _layer_spec((1, D_MODEL), layer)]
        + [_layer_spec((D_MODEL, w), layer) for w in widths],
        out_specs=[pl.BlockSpec((tm, w), rb) for w in widths],
        out_shape=[jax.ShapeDtypeStruct((rows, w), F32) for w in widths],
        compiler_params=_params(("parallel",)),
        name="inproj",
    )(x, g, wu, wqk, wv, wgc)


def _s5_param_body(lre_ref, lim_ref, ldt_ref, bre_ref, bim_ref, are_ref, aim_ref, bbre_ref, bbim_ref):
    lre = lre_ref[0]
    lim = lim_ref[0]
    dt = jnp.exp(ldt_ref[0])
    mag = jnp.exp(lre * dt)
    a_re = mag * jnp.cos(lim * dt)
    a_im = mag * jnp.sin(lim * dt)
    den = lre * lre + lim * lim
    pr = a_re - 1.0
    w_re = (pr * lre + a_im * lim) / den
    w_im = (a_im * lre - pr * lim) / den
    are_ref[0] = a_re
    aim_ref[0] = a_im
    bbre_ref[0] = w_re * bre_ref[0] - w_im * bim_ref[0]
    bbim_ref[0] = w_re * bim_ref[0] + w_im * bre_ref[0]


def _s5_params(lre, lim, ldt, bre, bim):
    depth = lre.shape[0]
    vec = pl.BlockSpec((1, 1, S5_LANES), lambda i: (i, 0, 0))
    mat = pl.BlockSpec((1, S5_GROUP, S5_LANES), lambda i: (i, 0, 0))
    return pl.pallas_call(
        _s5_param_body,
        grid=(depth,),
        in_specs=[vec, vec, vec, mat, mat],
        out_specs=[vec, vec, mat, mat],
        out_shape=[jax.ShapeDtypeStruct((depth, 1, S5_LANES), F32)] * 2
        + [jax.ShapeDtypeStruct((depth, S5_GROUP, S5_LANES), F32)] * 2,
        compiler_params=_params(("parallel",)),
        name="s5_params",
    )(lre, lim, ldt, bre, bim)


def _s5_body(*refs, n_batch, n_time, has_state, n_prev):
    u_ref, perm_ref, permt_ref, are_ref, aim_ref, bm_ref, cre_ref, cim_ref, d_ref = refs[:9]
    refs = refs[9:]
    if has_state:
        h0r_ref, h0i_ref = refs[:2]
        refs = refs[2:]
    y_ref, sr_ref, si_ref, hr_s, hi_s = refs[n_prev:]
    rows = n_batch * n_time

    @pl.when(pl.program_id(0) == 0)
    def _():
        if has_state:
            sr_ref[...] = h0r_ref[...]
            si_ref[...] = h0i_ref[...]
        else:
            sr_ref[...] = jnp.zeros_like(sr_ref)
            si_ref[...] = jnp.zeros_like(si_ref)

    u = u_ref[...].reshape(rows, S5_WIDTH)
    ub = _dot(perm_ref[...], u.astype(BF16)).astype(BF16)
    for j in range(S5_BLOCKS):
        r = _dot(ub[:, j * S5_BLOCK_IN:(j + 1) * S5_BLOCK_IN], bm_ref[j])
        hr_s[:, j * S5_BLOCK_ST:(j + 1) * S5_BLOCK_ST] = r[:, :S5_BLOCK_ST]
        hi_s[:, j * S5_BLOCK_ST:(j + 1) * S5_BLOCK_ST] = r[:, S5_BLOCK_ST:]

    for bt in range(n_batch // SUBLANES):
        for q in range(S5_LANES // S5_SCAN_LANES):
            lanes = pl.ds(q * S5_SCAN_LANES, S5_SCAN_LANES)
            brow = pl.ds(bt * SUBLANES, SUBLANES)
            a_re = jnp.broadcast_to(are_ref[:, lanes], (SUBLANES, S5_SCAN_LANES))
            a_im = jnp.broadcast_to(aim_ref[:, lanes], (SUBLANES, S5_SCAN_LANES))

            def step(t, carry, lanes=lanes, a_re=a_re, a_im=a_im, bt=bt):
                h_re, h_im = carry
                trow = pl.ds(pl.multiple_of(t * n_batch + bt * SUBLANES, SUBLANES), SUBLANES)
                n_re = a_re * h_re - a_im * h_im + hr_s[trow, lanes]
                n_im = a_re * h_im + a_im * h_re + hi_s[trow, lanes]
                hr_s[trow, lanes] = n_re
                hi_s[trow, lanes] = n_im
                return n_re, n_im

            h_re, h_im = lax.fori_loop(0, n_time, step, (sr_ref[brow, lanes], si_ref[brow, lanes]),
                                       unroll=2)
            sr_ref[brow, lanes] = h_re
            si_ref[brow, lanes] = h_im

    d = d_ref[...]
    permt = permt_ref[...]
    for j in range(S5_BLOCKS):
        st = slice(j * S5_BLOCK_ST, (j + 1) * S5_BLOCK_ST)
        ch = slice(j * S5_BLOCK_IN, (j + 1) * S5_BLOCK_IN)
        y = _dot(hr_s[:, st].astype(BF16), cre_ref[j]) - _dot(hi_s[:, st].astype(BF16), cim_ref[j])
        y_hi = y.astype(BF16)
        y_lo = (y - y_hi.astype(F32)).astype(BF16)
        y = _dot(permt, y_hi) + _dot(permt, y_lo)
        y_ref[:, :, ch] = (y + d[:, ch] * u[:, ch]).reshape(y_ref.shape[0], y_ref.shape[1], S5_BLOCK_IN)


def _s5(u, perm, permt, a_re, a_im, bm, cre, cim, d, h0, prev, layer, depth, n_batch, n_time):
    seq = u.shape[0] // n_batch
    rows = n_batch * n_time
    has_state = h0 is not None
    lead, blk_rows = (n_batch, n_time) if seq > n_time else (1, rows)
    u = u.reshape(lead, u.shape[0] // lead, S5_WIDTH)
    u_spec = pl.BlockSpec((lead, blk_rows, S5_WIDTH), lambda c: (0, c, 0))
    state_spec = pl.BlockSpec((None, n_batch, S5_LANES), lambda c: (layer, 0, 0))
    in_specs = [u_spec,
                _const_spec((rows, rows)), _const_spec((rows, rows)),
                _layer_spec((1, S5_LANES), layer), _layer_spec((1, S5_LANES), layer),
                _layer_spec((S5_BLOCKS, S5_BLOCK_IN, 2 * S5_BLOCK_ST), layer),
                _layer_spec((S5_BLOCKS, S5_BLOCK_ST, S5_BLOCK_IN), layer),
                _layer_spec((S5_BLOCKS, S5_BLOCK_ST, S5_BLOCK_IN), layer),
                _layer_spec((1, S5_WIDTH), layer)]
    args = [u, perm, permt, a_re, a_im, bm, cre, cim, d]
    if has_state:
        in_specs += [state_spec, state_spec]
        args += list(h0)
    aliases = {len(args) + k: 1 + k for k in range(len(prev))}
    in_specs += [_ANY] * len(prev)
    args += list(prev)
    return pl.pallas_call(
        functools.partial(_s5_body, n_batch=n_batch, n_time=n_time, has_state=has_state, n_prev=len(prev)),
        grid=(seq // n_time,),
        in_specs=in_specs,
        out_specs=[u_spec, state_spec, state_spec],
        out_shape=[jax.ShapeDtypeStruct(u.shape, F32),
                   jax.ShapeDtypeStruct((depth, n_batch, S5_LANES), F32),
                   jax.ShapeDtypeStruct((depth, n_batch, S5_LANES), F32)],
        scratch_shapes=[pltpu.VMEM((rows, S5_LANES), F32), pltpu.VMEM((rows, S5_LANES), F32)],
        input_output_aliases=aliases,
        compiler_params=_params(("arbitrary",)),
        name="s5",
    )(*args)


def _time_major_perm(n_batch, n_time):
    r = jnp.arange(n_batch * n_time)
    src = (r % n_batch) * n_time + r // n_batch
    return (src[:, None] == r[None, :]).astype(BF16)


def _mlstm_body(*refs, n_sub, seq_len, has_state, n_prev, single_chunk):
    qk_ref, v_ref, gc_ref, cw_ref, cb_ref, bc_ref, gm_ref = refs[:7]
    refs = refs[7:]
    if has_state:
        c0_ref, n0_ref, m0_ref, conv0_ref = refs[:4]
        refs = refs[4:]
    hm_ref, c_ref, n_ref, m_ref, conv_ref = refs[n_prev:n_prev + 5]
    scratch = refs[n_prev + 5:]
    xp_s = scratch[0]
    chunk = n_sub * seq_len
    region = xp_s.shape[0] // n_sub
    hist = SUBLANES

    from_input = has_state and single_chunk
    c_in, n_in, m_in = (c0_ref, n0_ref, m0_ref) if from_input else (c_ref, n_ref, m_ref)

    @pl.when(pl.program_id(1) == 0)
    def _():
        for j in range(n_sub):
            base = j * region
            if has_state:
                xp_s[base + hist - CONV_HIST:base + hist, :] = conv0_ref[j]
            else:
                xp_s[base + hist - CONV_HIST:base + hist, :] = jnp.zeros((CONV_HIST, 2 * MLSTM_WIDTH), F32)
        if not from_input:
            if has_state:
                c_ref[...] = c0_ref[...]
                n_ref[...] = n0_ref[...]
                m_ref[...] = m0_ref[...]
            else:
                c_ref[...] = jnp.zeros_like(c_ref)
                n_ref[...] = jnp.zeros_like(n_ref)
                m_ref[...] = jnp.zeros_like(m_ref)

    for j in range(n_sub):
        base = j * region
        xp_s[base + hist:base + hist + seq_len, :] = qk_ref[j * seq_len:(j + 1) * seq_len, :]
        acc = jnp.broadcast_to(cb_ref[...], (seq_len, 2 * MLSTM_WIDTH))
        for i in range(CONV_WIDTH):
            off = base + hist - CONV_HIST + i
            acc = acc + xp_s[off:off + seq_len, :] * cw_ref[i:i + 1, :]
        new_hist = xp_s[base + hist + seq_len - CONV_HIST:base + hist + seq_len, :]
        conv_ref[j] = new_hist
        if not single_chunk:
            xp_s[base + hist - CONV_HIST:base + hist, :] = new_hist
        if n_sub > 1:
            scratch[1][j * seq_len:(j + 1) * seq_len, :] = acc
    qk = jax.nn.silu(scratch[1][...] if n_sub > 1 else acc)
    q_all = qk[:, :MLSTM_WIDTH] * (MLSTM_HEAD_DIM ** -0.5)
    k_all = qk[:, MLSTM_WIDTH:]
    v_all = v_ref[...]

    def seq_of(idx):
        s = jnp.zeros(idx.shape, F32)
        for j in range(1, n_sub):
            s = s + jnp.where(idx >= j * seq_len, 1.0, 0.0)
        return s

    row_seq = seq_of(lax.broadcasted_iota(jnp.int32, (chunk, 1), 0))
    col_seq = seq_of(lax.broadcasted_iota(jnp.int32, (1, chunk), 1))
    t_id = lax.broadcasted_iota(jnp.int32, (chunk, chunk), 0)
    s_id = lax.broadcasted_iota(jnp.int32, (chunk, chunk), 1)
    same = jnp.where(row_seq == col_seq, 1.0, 0.0) if n_sub > 1 else jnp.ones((chunk, chunk), F32)
    tril = jnp.where(s_id <= t_id, same, 0.0)
    triu = jnp.where(t_id <= s_id, same, 0.0)
    causal = tril > 0.5
    pick = (lax.broadcasted_iota(jnp.int32, (SUBLANES, LANES), 0)
            == lax.broadcasted_iota(jnp.int32, (SUBLANES, LANES), 1)).astype(F32)

    li_col = gc_ref[...] + bc_ref[...]
    li_row = _dot_nt(pick, li_col, precision=HIGHEST)
    lf_col = jax.nn.log_sigmoid(li_col)
    lf_row = jax.nn.log_sigmoid(li_row)
    b_col = jnp.dot(tril, lf_col, preferred_element_type=F32, precision=HIGHEST)
    b_row = jnp.dot(lf_row, triu, preferred_element_type=F32, precision=HIGHEST)

    def per_row(vals):
        out = vals[0]
        for j in range(1, n_sub):
            out = jnp.where(row_seq == j, vals[j], out)
        return out

    m_new_parts = [[] for _ in range(n_sub)]
    for h in range(MLSTM_HEADS):
        hs = slice(h * MLSTM_HEAD_DIM, (h + 1) * MLSTM_HEAD_DIM)
        bc = b_col[:, MLSTM_HEADS + h:MLSTM_HEADS + h + 1]
        br = b_row[MLSTM_HEADS + h:MLSTM_HEADS + h + 1, :]
        lic = li_col[:, h:h + 1]
        lir = li_row[h:h + 1, :]
        m_prev = [m_in[j][:, h:h + 1] for j in range(n_sub)]
        c_old = [c_in[j, h] for j in range(n_sub)]
        n_old = [n_in[j, h:h + 1, :] for j in range(n_sub)]
        qf = q_all[:, hs]
        kf = k_all[:, hs]
        qb = qf.astype(BF16)
        vb = v_all[:, hs].astype(BF16)

        dmat = jnp.where(causal, bc - br + lir, -jnp.inf)
        inter = bc + per_row(m_prev)
        m_t = jnp.maximum(inter, jnp.max(dmat, axis=-1, keepdims=True))
        w = jnp.exp(dmat - m_t)
        scores = _dot_nt(qb, kf.astype(BF16)) * w
        a = jnp.exp(inter - m_t)
        q_c = per_row([_dot(qb, c_old[j].astype(BF16)) for j in range(n_sub)])
        num = _dot(scores.astype(BF16), vb) + a * q_c
        nq = (jnp.sum(scores, axis=-1, keepdims=True)
              + a * jnp.sum(qf * per_row(n_old), axis=-1, keepdims=True))
        hout = num / jnp.maximum(jnp.abs(nq), jnp.exp(-m_t))
        hout = hout * lax.rsqrt(jnp.mean(hout * hout, axis=-1, keepdims=True) + EPS)
        hm_ref[:, hs] = hout * gm_ref[:, hs]

        for j in range(n_sub):
            last = (j + 1) * seq_len - 1
            b_last = br[:, last:last + 1]
            g_row = b_last - br + lir
            if n_sub > 1:
                g_row = jnp.where(col_seq == j, g_row, -jnp.inf)
            m_new = jnp.maximum(b_last + m_prev[j], jnp.max(g_row, axis=-1, keepdims=True))
            wk = jnp.exp(b_last - bc + lic - m_new)
            if n_sub > 1:
                wk = jnp.where(row_seq == j, wk, 0.0)
            decay = jnp.exp(b_last + m_prev[j] - m_new)
            kw = kf * wk
            c_ref[j, h] = decay * c_old[j] + _dot_tn(kw.astype(BF16), vb)
            n_ref[j, h:h + 1, :] = decay * n_old[j] + jnp.sum(kw, axis=0, keepdims=True)
            m_new_parts[j].append(m_new)
    for j in range(n_sub):
        m_ref[j] = jnp.concatenate(m_new_parts[j], axis=1)


def _mlstm(qk, v, gc, cw, cb, bias_c, gm, state, prev, layer, depth, n_seq, n_sub, seq_len, n_chunk):
    assert n_sub == 1 or n_chunk == 1
    rows = qk.shape[0]
    hh, dh = MLSTM_HEADS, MLSTM_HEAD_DIM
    chunk = n_sub * seq_len
    has_state = state is not None
    rb = lambda b, c: (b * n_chunk + c, 0)
    st5 = lambda b, c: (layer, b, 0, 0, 0)
    st4 = lambda b, c: (layer, b, 0, 0)
    state_specs = [pl.BlockSpec((None, n_sub, hh, dh, dh), st5),
                   pl.BlockSpec((None, n_sub, hh, dh), st4),
                   pl.BlockSpec((None, n_sub, 1, hh), st4),
                   pl.BlockSpec((None, n_sub, CONV_HIST, 2 * MLSTM_WIDTH), st4)]
    in_specs = [pl.BlockSpec((chunk, 2 * MLSTM_WIDTH), rb),
                pl.BlockSpec((chunk, MLSTM_WIDTH), rb),
                pl.BlockSpec((chunk, LANES), rb),
                _layer_spec((CONV_WIDTH, 2 * MLSTM_WIDTH), layer), _layer_spec((1, 2 * MLSTM_WIDTH), layer),
                _layer_spec((1, LANES), layer), _layer_spec((1, MLSTM_WIDTH), layer)]
    args = [qk, v, gc, cw, cb, bias_c, gm]
    if has_state:
        in_specs += state_specs
        args += list(state)
    aliases = {len(args) + k: 1 + k for k in range(len(prev))}
    in_specs += [_ANY] * len(prev)
    args += list(prev)
    n_all = n_seq * n_sub
    out_shape = [jax.ShapeDtypeStruct((rows, MLSTM_WIDTH), F32),
                 jax.ShapeDtypeStruct((depth, n_all, hh, dh, dh), F32),
                 jax.ShapeDtypeStruct((depth, n_all, hh, dh), F32),
                 jax.ShapeDtypeStruct((depth, n_all, 1, hh), F32),
                 jax.ShapeDtypeStruct((depth, n_all, CONV_HIST, 2 * MLSTM_WIDTH), F32)]
    region = SUBLANES + -(-seq_len // SUBLANES) * SUBLANES
    scratch = [pltpu.VMEM((n_sub * region, 2 * MLSTM_WIDTH), F32)]
    if n_sub > 1:
        scratch.append(pltpu.VMEM((chunk, 2 * MLSTM_WIDTH), F32))
    return pl.pallas_call(
        functools.partial(_mlstm_body, n_sub=n_sub, seq_len=seq_len, has_state=has_state,
                          n_prev=len(prev), single_chunk=n_chunk == 1),
        grid=(n_seq, n_chunk),
        in_specs=in_specs,
        out_specs=[pl.BlockSpec((chunk, MLSTM_WIDTH), rb)] + state_specs,
        out_shape=out_shape,
        scratch_shapes=scratch,
        input_output_aliases=aliases,
        compiler_params=_params(("parallel", "arbitrary")),
        name="mlstm",
    )(*args)


def _post_body(x_ref, y_ref, hm_ref, g_ref, wo_ref, wgs_ref, wgm_ref, wglu_ref, wsu_ref, wmu_ref,
               wout_ref, o_ref):
    x = x_ref[...]
    h = _rms(x, g_ref[...]).astype(BF16)
    ys = jax.nn.gelu(y_ref[...])
    ys = ys * jax.nn.sigmoid(_dot(ys.astype(BF16), wglu_ref[...]))
    hm = hm_ref[...] * jax.nn.sigmoid(_dot(h, wo_ref[...]))
    merged = (jax.nn.sigmoid(_dot(h, wgs_ref[...])) * _dot(ys.astype(BF16), wsu_ref[...])
              + jax.nn.sigmoid(_dot(h, wgm_ref[...])) * _dot(hm.astype(BF16), wmu_ref[...]))
    o_ref[...] = x + _dot(merged.astype(BF16), wout_ref[...])


def _post(x, y, hm, g, wo, wgs, wgm, wglu, wsu, wmu, wout, layer):
    rows = x.shape[0]
    tm = min(ROW_TILE, rows)
    rb = lambda i: (i, 0)
    sq = _layer_spec((D_MODEL, D_MODEL), layer)
    return pl.pallas_call(
        _post_body,
        grid=(rows // tm,),
        in_specs=[pl.BlockSpec((tm, D_MODEL), rb), pl.BlockSpec((tm, S5_WIDTH), rb),
                  pl.BlockSpec((tm, MLSTM_WIDTH), rb),
                  _layer_spec((1, D_MODEL), layer), sq, sq, sq,
                  _layer_spec((S5_WIDTH, S5_WIDTH), layer), _layer_spec((S5_WIDTH, D_MODEL), layer), sq, sq],
        out_specs=pl.BlockSpec((tm, D_MODEL), rb),
        out_shape=jax.ShapeDtypeStruct((rows, D_MODEL), F32),
        compiler_params=_params(("parallel",)),
        name="post",
    )(x, y, hm, g, wo, wgs, wgm, wglu, wsu, wmu, wout)


def _block_diag(t):
    depth, nb, gpb, a, c = t.shape
    eye = jnp.eye(gpb, dtype=t.dtype)
    return jnp.einsum("ljgac,gk->ljgakc", t, eye).reshape(depth, nb, gpb * a, gpb * c)


def kernel(x_prompt, x_sample, state_s5_re, state_s5_im, state_mlstm_c, state_mlstm_n, state_mlstm_m, state_conv, p_prompt, p_sample, g_ffn1, w1_gate, w1_up, w1_down, g_mix, w_in, s5_lambda_re, s5_lambda_im, s5_log_dt, s5_b_re, s5_b_im, s5_c_re, s5_c_im, s5_d, s5_w_glu, w_s5_up, conv_w, conv_b, b_igate, b_fgate, g_mhead, w_m_up, w_out, g_ffn2, w2_gate, w2_up, w2_down, g_ple, w_ple, w_ple_gate, g_final):
    depth = w_in.shape[0]
    bp, sp, _ = x_prompt.shape
    bs, ss, _ = x_sample.shape
    hh = MLSTM_HEADS
    gpb = S5_GROUPS // S5_BLOCKS

    bf = lambda w: w.astype(BF16)
    cuts = [0, S5_WIDTH, S5_WIDTH + 2 * MLSTM_WIDTH, S5_WIDTH + 3 * MLSTM_WIDTH, S5_WIDTH + 4 * MLSTM_WIDTH]
    c_gate = cuts[4] + 2 * MLSTM_HEADS
    w_u = bf(w_in[:, :, cuts[0]:cuts[1]])
    w_qk = bf(w_in[:, :, cuts[1]:cuts[2]])
    w_v = bf(w_in[:, :, cuts[2]:cuts[3]])
    w_o = bf(w_in[:, :, cuts[3]:cuts[4]])
    w_gc = bf(jnp.pad(w_in[:, :, cuts[4]:c_gate], ((0, 0), (0, 0), (0, LANES - 2 * MLSTM_HEADS))))
    w_gs = bf(w_in[:, :, c_gate:c_gate + D_MODEL])
    w_gm = bf(w_in[:, :, c_gate + D_MODEL:])
    w1g, w1u, w1d = bf(w1_gate), bf(w1_up), bf(w1_down)
    w2g, w2u, w2d = bf(w2_gate), bf(w2_up), bf(w2_down)
    wglu, wsu, wmu, wout = bf(s5_w_glu), bf(w_s5_up), bf(w_m_up), bf(w_out)
    wple, wpg = bf(w_ple), bf(w_ple_gate)
    row = lambda g: g.reshape(depth, 1, -1)
    gf1, gmx, gf2, gpl, gmh, s5d, cvb = (row(g_ffn1), row(g_mix), row(g_ffn2), row(g_ple),
                                          row(g_mhead), row(s5_d), row(conv_b))
    gfin = g_final.reshape(1, D_MODEL)
    bias_c = jnp.pad(jnp.concatenate([b_igate, b_fgate], axis=1),
                     ((0, 0), (0, LANES - 2 * MLSTM_HEADS))).reshape(depth, 1, LANES)

    lanes3 = lambda t: t.reshape(depth, 1, S5_LANES)
    ldt = jnp.broadcast_to(s5_log_dt[:, :, None], (depth, S5_GROUPS, S5_STATE))
    to_cols = lambda t: jnp.transpose(t, (0, 3, 1, 2)).reshape(depth, S5_GROUP, S5_LANES)
    a_re, a_im, bb_re, bb_im = _s5_params(lanes3(s5_lambda_re), lanes3(s5_lambda_im), lanes3(ldt),
                                          to_cols(s5_b_re), to_cols(s5_b_im))
    to_blk = lambda t: jnp.transpose(t.reshape(depth, S5_GROUP, S5_BLOCKS, gpb, S5_STATE), (0, 2, 3, 1, 4))
    bmat = bf(jnp.concatenate([_block_diag(to_blk(bb_re)), _block_diag(to_blk(bb_im))], axis=-1))
    c_blk = lambda t: jnp.transpose(t.reshape(depth, S5_BLOCKS, gpb, S5_GROUP, S5_STATE), (0, 1, 2, 4, 3))
    cmat_re = bf(_block_diag(c_blk(s5_c_re)))
    cmat_im = bf(_block_diag(c_blk(s5_c_im)))
    s5_time_p = S5_ROWS // bp
    perm_p = _time_major_perm(bp, s5_time_p)
    perm_s = _time_major_perm(bs, ss)

    xp = x_prompt.reshape(bp * sp, D_MODEL)
    xs = x_sample.reshape(bs * ss, D_MODEL)
    pp = p_prompt.reshape(depth, bp * sp, D_PLE)
    ps = p_sample.reshape(depth, bs * ss, D_PLE)
    s5_state = (state_s5_re.reshape(depth, bs, S5_LANES), state_s5_im.reshape(depth, bs, S5_LANES))
    lstm_state = (state_mlstm_c, state_mlstm_n, state_mlstm_m.reshape(depth, bs, 1, hh), state_conv)

    s5_out_p, s5_out_s, lstm_out_p, lstm_out_s = (), (), (), ()
    for i in range(depth):
        final = i == depth - 1
        ffn1_w = (gf1, w1g, w1u, w1d)
        proj_w = (gmx, w_u, w_qk, w_v, w_gc)
        s5_w = (a_re, a_im, bmat, cmat_re, cmat_im, s5d)
        lstm_w = (conv_w, cvb, bias_c, gmh)
        mix_w = (gmx, w_o, w_gs, w_gm, wglu, wsu, wmu, wout)
        ple_w = (gf2, w2g, w2u, w2d, gpl, wple, wpg, gfin)

        xp = _ffn(xp, *ffn1_w, layer=i)
        u, qk, v, gc = _inproj(xp, *proj_w, layer=i)
        y, *s5_out_p = _s5(u, perm_p, perm_p.T, *s5_w, None, s5_out_p,
                           layer=i, depth=depth, n_batch=bp, n_time=s5_time_p)
        hm, *lstm_out_p = _mlstm(qk, v, gc, *lstm_w, None, lstm_out_p, layer=i, depth=depth,
                                 n_seq=bp, n_sub=1, seq_len=MLSTM_CHUNK, n_chunk=sp // MLSTM_CHUNK)
        xp = _post(xp, y.reshape(bp * sp, S5_WIDTH), hm, *mix_w, layer=i)
        xp = _ffn_ple(xp, pp, *ple_w, layer=i, final=final)

        xs = _ffn(xs, *ffn1_w, layer=i)
        u, qk, v, gc = _inproj(xs, *proj_w, layer=i)
        y, *s5_out_s = _s5(u, perm_s, perm_s.T, *s5_w, s5_state, s5_out_s,
                           layer=i, depth=depth, n_batch=bs, n_time=ss)
        hm, *lstm_out_s = _mlstm(qk, v, gc, *lstm_w, lstm_state, lstm_out_s, layer=i, depth=depth,
                                 n_seq=bs // SAMPLE_SEQS_PER_STEP, n_sub=SAMPLE_SEQS_PER_STEP,
                                 seq_len=ss, n_chunk=1)
        xs = _post(xs, y.reshape(bs * ss, S5_WIDTH), hm, *mix_w, layer=i)
        xs = _ffn_ple(xs, ps, *ple_w, layer=i, final=final)

    def states(n, s5_out, lstm_out):
        c_n, n_n, m_n, conv_n = lstm_out
        return (s5_out[0].reshape(depth, n, S5_GROUPS, S5_STATE), s5_out[1].reshape(depth, n, S5_GROUPS, S5_STATE),
                c_n, n_n, m_n.reshape(depth, n, hh), conv_n)

    return ((xp.reshape(bp, sp, D_MODEL), xs.reshape(bs, ss, D_MODEL))
            + states(bp, s5_out_p, lstm_out_p) + states(bs, s5_out_s, lstm_out_s))
```

```python
import functools

import jax
import jax.numpy as jnp
from jax import lax
from jax.experimental import pallas as pl
from jax.experimental.pallas import tpu as pltpu

D_MODEL = 1024
D_PLE = 256
D_FF = 2048
S5_WIDTH = 512
S5_GROUP = 16
S5_GROUPS = S5_WIDTH // S5_GROUP
S5_STATE = 64
S5_LANES = S5_GROUPS * S5_STATE
MLSTM_WIDTH = 1024
MLSTM_HEADS = 4
MLSTM_HEAD_DIM = MLSTM_WIDTH // MLSTM_HEADS
MLSTM_CHUNK = 128
CONV_WIDTH = 4
CONV_HIST = CONV_WIDTH - 1
EPS = 1e-6

F32 = jnp.float32
BF16 = jnp.bfloat16
HIGHEST = lax.Precision.HIGHEST

SUBLANES = 8
LANES = 128
VMEM_LIMIT_BYTES = 56 * 1024 * 1024

ROW_TILE = 512
S5_ROWS = 512
S5_BLOCKS = 4
S5_BLOCK_IN = S5_WIDTH // S5_BLOCKS
S5_BLOCK_ST = S5_LANES // S5_BLOCKS
SAMPLE_SEQS_PER_TILE = 2
MLSTM_TILES_PER_STEP = 2


def _params(sem):
    return pltpu.CompilerParams(dimension_semantics=sem, vmem_limit_bytes=VMEM_LIMIT_BYTES)


def _layer_spec(shape, layer):
    nd = len(shape)
    return pl.BlockSpec((None,) + tuple(shape), lambda *_: (layer,) + (0,) * nd,
                        pipeline_mode=pl.Buffered(1))


def _const_spec(shape):
    nd = len(shape)
    return pl.BlockSpec(tuple(shape), lambda *_: (0,) * nd, pipeline_mode=pl.Buffered(1))


_ANY = pl.BlockSpec(memory_space=pl.ANY)


def _rms(x, g):
    return x * lax.rsqrt(jnp.mean(x * x, axis=-1, keepdims=True) + EPS) * g


def _dot(a, b):
    return jnp.dot(a, b, preferred_element_type=F32)


def _dot_nt(a, b, **kw):
    return lax.dot_general(a, b, (((1,), (1,)), ((), ())), preferred_element_type=F32, **kw)


def _dot_tn(a, b):
    return lax.dot_general(a, b, (((0,), (0,)), ((), ())), preferred_element_type=F32)


def _ffn_body(x_ref, g_ref, wg_ref, wu_ref, wd_ref, o_ref):
    x = x_ref[...]
    h = _rms(x, g_ref[...]).astype(BF16)
    z = (jax.nn.silu(_dot(h, wg_ref[...])) * _dot(h, wu_ref[...])).astype(BF16)
    o_ref[...] = x + 0.5 * _dot(z, wd_ref[...])


def _ffn_ple_body(x_ref, p_ref, g_ref, wg_ref, wu_ref, wd_ref, gp_ref, wp_ref, wpg_ref, gf_ref, o_ref,
                  *, final):
    x = x_ref[...]
    h = _rms(x, g_ref[...]).astype(BF16)
    z = (jax.nn.silu(_dot(h, wg_ref[...])) * _dot(h, wu_ref[...])).astype(BF16)
    x = x + 0.5 * _dot(z, wd_ref[...])
    gate = jax.nn.sigmoid(_dot(_rms(x, gp_ref[...]).astype(BF16), wpg_ref[...]))
    x = x + _dot(p_ref[...].astype(BF16), wp_ref[...]) * gate
    if final:
        x = _rms(x, gf_ref[...])
    o_ref[...] = x


def _ffn(x, g, wg, wu, wd, layer):
    rows = x.shape[0]
    tm = min(ROW_TILE, rows)
    row_spec = pl.BlockSpec((tm, D_MODEL), lambda i: (i, 0))
    return pl.pallas_call(
        _ffn_body,
        grid=(rows // tm,),
        in_specs=[row_spec, _layer_spec((1, D_MODEL), layer), _layer_spec((D_MODEL, D_FF), layer),
                  _layer_spec((D_MODEL, D_FF), layer), _layer_spec((D_FF, D_MODEL), layer)],
        out_specs=row_spec,
        out_shape=jax.ShapeDtypeStruct((rows, D_MODEL), F32),
        compiler_params=_params(("parallel",)),
        name="ffn",
    )(x, g, wg, wu, wd)


def _ffn_ple(x, p, g, wg, wu, wd, gp, wp, wpg, gf, layer, final):
    rows = x.shape[0]
    tm = min(ROW_TILE, rows)
    row_spec = pl.BlockSpec((tm, D_MODEL), lambda i: (i, 0))
    return pl.pallas_call(
        functools.partial(_ffn_ple_body, final=final),
        grid=(rows // tm,),
        in_specs=[row_spec, pl.BlockSpec((None, tm, D_PLE), lambda i: (layer, i, 0)),
                  _layer_spec((1, D_MODEL), layer), _layer_spec((D_MODEL, D_FF), layer),
                  _layer_spec((D_MODEL, D_FF), layer), _layer_spec((D_FF, D_MODEL), layer),
                  _layer_spec((1, D_MODEL), layer), _layer_spec((D_PLE, D_MODEL), layer),
                  _layer_spec((D_MODEL, D_MODEL), layer), _const_spec((1, D_MODEL))],
        out_specs=row_spec,
        out_shape=jax.ShapeDtypeStruct((rows, D_MODEL), F32),
        compiler_params=_params(("parallel",)),
        name="ffn_ple",
    )(x, p, g, wg, wu, wd, gp, wp, wpg, gf)


def _inproj_body(x_ref, g_ref, wu_ref, wqk_ref, wv_ref, wgc_ref, u_ref, qk_ref, v_ref, gc_ref):
    h = _rms(x_ref[...], g_ref[...]).astype(BF16)
    u_ref[...] = _dot(h, wu_ref[...])
    qk_ref[...] = _dot(h, wqk_ref[...])
    v_ref[...] = _dot(h, wv_ref[...])
    gc_ref[...] = _dot(h, wgc_ref[...])


def _inproj(x, g, wu, wqk, wv, wgc, layer):
    rows = x.shape[0]
    tm = min(ROW_TILE, rows)
    rb = lambda i: (i, 0)
    widths = (S5_WIDTH, 2 * MLSTM_WIDTH, MLSTM_WIDTH, LANES)
    return pl.pallas_call(
        _inproj_body,
        grid=(rows // tm,),
        in_specs=[pl.BlockSpec((tm, D_MODEL), rb), _layer_spec((1, D_MODEL), layer)]
        + [_layer_spec((D_MODEL, w), layer) for w in widths],
        out_specs=[pl.BlockSpec((tm, w), rb) for w in widths],
        out_shape=[jax.ShapeDtypeStruct((rows, w), F32) for w in widths],
        compiler_params=_params(("parallel",)),
        name="inproj",
    )(x, g, wu, wqk, wv, wgc)


def _s5_param_body(lre_ref, lim_ref, ldt_ref, bre_ref, bim_ref, are_ref, aim_ref, bbre_ref, bbim_ref):
    lre = lre_ref[0]
    lim = lim_ref[0]
    dt = jnp.exp(ldt_ref[0])
    mag = jnp.exp(lre * dt)
    a_re = mag * jnp.cos(lim * dt)
    a_im = mag * jnp.sin(lim * dt)
    den = lre * lre + lim * lim
    pr = a_re - 1.0
    w_re = (pr * lre + a_im * lim) / den
    w_im = (a_im * lre - pr * lim) / den
    are_ref[0] = a_re
    aim_ref[0] = a_im
    bbre_ref[0] = w_re * bre_ref[0] - w_im * bim_ref[0]
    bbim_ref[0] = w_re * bim_ref[0] + w_im * bre_ref[0]


def _s5_params(lre, lim, ldt, bre, bim):
    depth = lre.shape[0]
    vec = pl.BlockSpec((1, 1, S5_LANES), lambda i: (i, 0, 0))
    mat = pl.BlockSpec((1, S5_GROUP, S5_LANES), lambda i: (i, 0, 0))
    return pl.pallas_call(
        _s5_param_body,
        grid=(depth,),
        in_specs=[vec, vec, vec, mat, mat],
        out_specs=[vec, vec, mat, mat],
        out_shape=[jax.ShapeDtypeStruct((depth, 1, S5_LANES), F32)] * 2
        + [jax.ShapeDtypeStruct((depth, S5_GROUP, S5_LANES), F32)] * 2,
        compiler_params=_params(("parallel",)),
        name="s5_params",
    )(lre, lim, ldt, bre, bim)


def _s5_body(*refs, n_batch, n_time, has_state, n_prev):
    u_ref, perm_ref, permt_ref, are_ref, aim_ref, bm_ref, cre_ref, cim_ref, d_ref = refs[:9]
    refs = refs[9:]
    if has_state:
        h0r_ref, h0i_ref = refs[:2]
        refs = refs[2:]
    y_ref, sr_ref, si_ref, ub_s, bur_s, bui_s, hr_s, hi_s, y_s = refs[n_prev:]
    rows = n_batch * n_time
    n_bt = n_batch // SUBLANES

    @pl.when(pl.program_id(0) == 0)
    def _():
        if has_state:
            sr_ref[...] = h0r_ref[...]
            si_ref[...] = h0i_ref[...]
        else:
            sr_ref[...] = jnp.zeros_like(sr_ref)
            si_ref[...] = jnp.zeros_like(si_ref)

    u = u_ref[...].reshape(rows, S5_WIDTH)
    ub = _dot(perm_ref[...], u.astype(BF16)).astype(BF16)
    for j in range(S5_BLOCKS):
        ub_s[j] = ub[:, j * S5_BLOCK_IN:(j + 1) * S5_BLOCK_IN]

    def project_in(j):
        r = _dot(ub_s[j], bm_ref[j])
        bur_s[j] = r[:, :S5_BLOCK_ST]
        bui_s[j] = r[:, S5_BLOCK_ST:]

    def scan(j):
        a_re = jnp.broadcast_to(are_ref[j], (SUBLANES, S5_BLOCK_ST))
        a_im = jnp.broadcast_to(aim_ref[j], (SUBLANES, S5_BLOCK_ST))

        def advance(h, row):
            h_re, h_im = h
            tile = slice(row, row + SUBLANES)
            return (a_re * h_re - a_im * h_im + bur_s[j, tile, :],
                    a_re * h_im + a_im * h_re + bui_s[j, tile, :])

        def emit(row, first, second):
            pair = slice(row, row + 2 * SUBLANES)
            hr_s[j, pair, :] = jnp.concatenate([first[0], second[0]], axis=0).astype(BF16)
            hi_s[j, pair, :] = jnp.concatenate([first[1], second[1]], axis=0).astype(BF16)

        if n_bt == 1:
            h = (sr_ref[j], si_ref[j])
            for t in range(0, n_time, 2):
                h1 = advance(h, t * SUBLANES)
                h = advance(h1, (t + 1) * SUBLANES)
                emit(t * SUBLANES, h1, h)
            sr_ref[j], si_ref[j] = h
        else:
            for bt in range(0, n_bt, 2):
                rows_a = slice(bt * SUBLANES, (bt + 1) * SUBLANES)
                rows_b = slice((bt + 1) * SUBLANES, (bt + 2) * SUBLANES)
                ha = (sr_ref[j, rows_a, :], si_ref[j, rows_a, :])
                hb = (sr_ref[j, rows_b, :], si_ref[j, rows_b, :])
                for t in range(n_time):
                    row = t * n_batch + bt * SUBLANES
                    ha = advance(ha, row)
                    hb = advance(hb, row + SUBLANES)
                    emit(row, ha, hb)
                sr_ref[j, rows_a, :], si_ref[j, rows_a, :] = ha
                sr_ref[j, rows_b, :], si_ref[j, rows_b, :] = hb

    def project_out(j):
        y_s[j] = _dot(hr_s[j], cre_ref[j]) - _dot(hi_s[j], cim_ref[j])

    project_in(0)

    def pipelined(j, carry):
        project_in(j + 1)
        scan(j)
        project_out(j)
        return carry

    lax.fori_loop(0, S5_BLOCKS - 1, pipelined, 0)
    scan(S5_BLOCKS - 1)
    project_out(S5_BLOCKS - 1)
    y = jnp.concatenate([y_s[j] for j in range(S5_BLOCKS)], axis=1)
    y_hi = y.astype(BF16)
    y_lo = (y - y_hi.astype(F32)).astype(BF16)
    permt = permt_ref[...]
    y = _dot(permt, y_hi) + _dot(permt, y_lo) + d_ref[...] * u
    y_ref[...] = y.reshape(y_ref.shape)


def _s5(u, perm, permt, a_re, a_im, bm, cre, cim, d, h0, prev, layer, depth, n_batch, n_time):
    seq = u.shape[0] // n_batch
    rows = n_batch * n_time
    has_state = h0 is not None
    assert n_batch % (2 * SUBLANES) == 0 or (n_batch == SUBLANES and n_time % 2 == 0)
    lead, blk_rows = (n_batch, n_time) if seq > n_time else (1, rows)
    u = u.reshape(lead, u.shape[0] // lead, S5_WIDTH)
    u_spec = pl.BlockSpec((lead, blk_rows, S5_WIDTH), lambda c: (0, c, 0))
    state_spec = pl.BlockSpec((None, S5_BLOCKS, n_batch, S5_BLOCK_ST), lambda c: (layer, 0, 0, 0))
    state_shape = jax.ShapeDtypeStruct((depth, S5_BLOCKS, n_batch, S5_BLOCK_ST), F32)
    in_specs = [u_spec,
                _const_spec((rows, rows)), _const_spec((rows, rows)),
                _layer_spec((S5_BLOCKS, 1, S5_BLOCK_ST), layer), _layer_spec((S5_BLOCKS, 1, S5_BLOCK_ST), layer),
                _layer_spec((S5_BLOCKS, S5_BLOCK_IN, 2 * S5_BLOCK_ST), layer),
                _layer_spec((S5_BLOCKS, S5_BLOCK_ST, S5_BLOCK_IN), layer),
                _layer_spec((S5_BLOCKS, S5_BLOCK_ST, S5_BLOCK_IN), layer),
                _layer_spec((1, S5_WIDTH), layer)]
    args = [u, perm, permt, a_re, a_im, bm, cre, cim, d]
    if has_state:
        in_specs += [state_spec, state_spec]
        args += list(h0)
    aliases = {len(args) + k: 1 + k for k in range(len(prev))}
    in_specs += [_ANY] * len(prev)
    args += list(prev)
    return pl.pallas_call(
        functools.partial(_s5_body, n_batch=n_batch, n_time=n_time, has_state=has_state, n_prev=len(prev)),
        grid=(seq // n_time,),
        in_specs=in_specs,
        out_specs=[u_spec, state_spec, state_spec],
        out_shape=[jax.ShapeDtypeStruct(u.shape, F32), state_shape, state_shape],
        scratch_shapes=[pltpu.VMEM((S5_BLOCKS, rows, S5_BLOCK_IN), BF16),
                        pltpu.VMEM((S5_BLOCKS, rows, S5_BLOCK_ST), F32),
                        pltpu.VMEM((S5_BLOCKS, rows, S5_BLOCK_ST), F32),
                        pltpu.VMEM((S5_BLOCKS, rows, S5_BLOCK_ST), BF16),
                        pltpu.VMEM((S5_BLOCKS, rows, S5_BLOCK_ST), BF16),
                        pltpu.VMEM((S5_BLOCKS, rows, S5_BLOCK_IN), F32)],
        input_output_aliases=aliases,
        compiler_params=_params(("arbitrary",)),
        name="s5",
    )(*args)


def _time_major_perm(n_batch, n_time):
    r = jnp.arange(n_batch * n_time)
    src = (r % n_batch) * n_time + r // n_batch
    return (src[:, None] == r[None, :]).astype(BF16)


def _mlstm_tile(g, qk_ref, v_ref, gc_ref, cw_ref, cb_ref, bc_ref, gm_ref, conv0_ref,
                c_in, n_in, m_in, hm_ref, c_ref, n_ref, m_ref, conv_ref, xp_s, qk_s,
                *, n_sub, seq_len, carry_hist):
    chunk = n_sub * seq_len
    region = xp_s.shape[0] // (MLSTM_TILES_PER_STEP * n_sub)
    hist = SUBLANES
    slots = [g * n_sub + j for j in range(n_sub)]

    for j, slot in enumerate(slots):
        base = slot * region
        xp_s[base + hist:base + hist + seq_len, :] = qk_ref[g, j * seq_len:(j + 1) * seq_len, :]
        acc = jnp.broadcast_to(cb_ref[...], (seq_len, 2 * MLSTM_WIDTH))
        for i in range(CONV_WIDTH):
            off = base + hist - CONV_HIST + i
            acc = acc + xp_s[off:off + seq_len, :] * cw_ref[i:i + 1, :]
        new_hist = xp_s[base + hist + seq_len - CONV_HIST:base + hist + seq_len, :]
        conv_ref[slot] = new_hist
        if carry_hist:
            xp_s[base + hist - CONV_HIST:base + hist, :] = new_hist
        if n_sub > 1:
            qk_s[g, j * seq_len:(j + 1) * seq_len, :] = acc
    qk = jax.nn.silu(qk_s[g] if n_sub > 1 else acc)
    q_all = qk[:, :MLSTM_WIDTH] * (MLSTM_HEAD_DIM ** -0.5)
    k_all = qk[:, MLSTM_WIDTH:]
    v_all = v_ref[g]

    def seq_of(idx):
        s = jnp.zeros(idx.shape, F32)
        for j in range(1, n_sub):
            s = s + jnp.where(idx >= j * seq_len, 1.0, 0.0)
        return s

    row_seq = seq_of(lax.broadcasted_iota(jnp.int32, (chunk, 1), 0))
    col_seq = seq_of(lax.broadcasted_iota(jnp.int32, (1, chunk), 1))
    t_id = lax.broadcasted_iota(jnp.int32, (chunk, chunk), 0)
    s_id = lax.broadcasted_iota(jnp.int32, (chunk, chunk), 1)
    same = jnp.where(row_seq == col_seq, 1.0, 0.0) if n_sub > 1 else jnp.ones((chunk, chunk), F32)
    tril = jnp.where(s_id <= t_id, same, 0.0)
    triu = jnp.where(t_id <= s_id, same, 0.0)
    causal = tril > 0.5
    pick = (lax.broadcasted_iota(jnp.int32, (SUBLANES, LANES), 0)
            == lax.broadcasted_iota(jnp.int32, (SUBLANES, LANES), 1)).astype(F32)

    li_col = gc_ref[g] + bc_ref[...]
    li_row = _dot_nt(pick, li_col, precision=HIGHEST)
    lf_col = jax.nn.log_sigmoid(li_col)
    lf_row = jax.nn.log_sigmoid(li_row)
    b_col = jnp.dot(tril, lf_col, preferred_element_type=F32, precision=HIGHEST)
    b_row = jnp.dot(lf_row, triu, preferred_element_type=F32, precision=HIGHEST)

    def per_row(vals):
        out = vals[0]
        for j in range(1, n_sub):
            out = jnp.where(row_seq == j, vals[j], out)
        return out

    m_new_parts = [[] for _ in range(n_sub)]
    for h in range(MLSTM_HEADS):
        hs = slice(h * MLSTM_HEAD_DIM, (h + 1) * MLSTM_HEAD_DIM)
        bc = b_col[:, MLSTM_HEADS + h:MLSTM_HEADS + h + 1]
        br = b_row[MLSTM_HEADS + h:MLSTM_HEADS + h + 1, :]
        lic = li_col[:, h:h + 1]
        lir = li_row[h:h + 1, :]
        m_prev = [m_in[s][:, h:h + 1] for s in slots]
        c_old = [c_in[s, h] for s in slots]
        n_old = [n_in[s, h:h + 1, :] for s in slots]
        qf = q_all[:, hs]
        kf = k_all[:, hs]
        qb = qf.astype(BF16)
        vb = v_all[:, hs].astype(BF16)

        dmat = jnp.where(causal, bc - br + lir, -jnp.inf)
        inter = bc + per_row(m_prev)
        m_t = jnp.maximum(inter, jnp.max(dmat, axis=-1, keepdims=True))
        w = jnp.exp(dmat - m_t)
        scores = _dot_nt(qb, kf.astype(BF16)) * w
        a = jnp.exp(inter - m_t)
        q_c = per_row([_dot(qb, c.astype(BF16)) for c in c_old])
        num = _dot(scores.astype(BF16), vb) + a * q_c
        nq = (jnp.sum(scores, axis=-1, keepdims=True)
              + a * jnp.sum(qf * per_row(n_old), axis=-1, keepdims=True))
        hout = num / jnp.maximum(jnp.abs(nq), jnp.exp(-m_t))
        hout = hout * lax.rsqrt(jnp.mean(hout * hout, axis=-1, keepdims=True) + EPS)
        hm_ref[g, :, hs] = hout * gm_ref[:, hs]

        for j, slot in enumerate(slots):
            last = (j + 1) * seq_len - 1
            b_last = br[:, last:last + 1]
            g_row = b_last - br + lir
            if n_sub > 1:
                g_row = jnp.where(col_seq == j, g_row, -jnp.inf)
            m_new = jnp.maximum(b_last + m_prev[j], jnp.max(g_row, axis=-1, keepdims=True))
            wk = jnp.exp(b_last - bc + lic - m_new)
            if n_sub > 1:
                wk = jnp.where(row_seq == j, wk, 0.0)
            decay = jnp.exp(b_last + m_prev[j] - m_new)
            kw = kf * wk
            c_ref[slot, h] = decay * c_old[j] + _dot_tn(kw.astype(BF16), vb)
            n_ref[slot, h:h + 1, :] = decay * n_old[j] + jnp.sum(kw, axis=0, keepdims=True)
            m_new_parts[j].append(m_new)
    for j, slot in enumerate(slots):
        m_ref[slot] = jnp.concatenate(m_new_parts[j], axis=1)


def _mlstm_body(*refs, n_sub, seq_len, has_state, n_prev, single_chunk):
    qk_ref, v_ref, gc_ref, cw_ref, cb_ref, bc_ref, gm_ref = refs[:7]
    refs = refs[7:]
    c0_ref = n0_ref = m0_ref = conv0_ref = None
    if has_state:
        c0_ref, n0_ref, m0_ref, conv0_ref = refs[:4]
        refs = refs[4:]
    hm_ref, c_ref, n_ref, m_ref, conv_ref = refs[n_prev:n_prev + 5]
    scratch = refs[n_prev + 5:]
    xp_s = scratch[0]
    qk_s = scratch[1] if n_sub > 1 else None
    n_slots = MLSTM_TILES_PER_STEP * n_sub
    region = xp_s.shape[0] // n_slots
    hist = SUBLANES

    from_input = has_state and single_chunk
    c_in, n_in, m_in = (c0_ref, n0_ref, m0_ref) if from_input else (c_ref, n_ref, m_ref)

    @pl.when(pl.program_id(1) == 0)
    def _():
        for slot in range(n_slots):
            base = slot * region
            if has_state:
                xp_s[base + hist - CONV_HIST:base + hist, :] = conv0_ref[slot]
            else:
                xp_s[base + hist - CONV_HIST:base + hist, :] = jnp.zeros((CONV_HIST, 2 * MLSTM_WIDTH), F32)
        if not from_input:
            if has_state:
                c_ref[...] = c0_ref[...]
                n_ref[...] = n0_ref[...]
                m_ref[...] = m0_ref[...]
            else:
                c_ref[...] = jnp.zeros_like(c_ref)
                n_ref[...] = jnp.zeros_like(n_ref)
                m_ref[...] = jnp.zeros_like(m_ref)

    for g in range(MLSTM_TILES_PER_STEP):
        _mlstm_tile(g, qk_ref, v_ref, gc_ref, cw_ref, cb_ref, bc_ref, gm_ref, conv0_ref,
                    c_in, n_in, m_in, hm_ref, c_ref, n_ref, m_ref, conv_ref, xp_s, qk_s,
                    n_sub=n_sub, seq_len=seq_len, carry_hist=not single_chunk)


def _mlstm(qk, v, gc, cw, cb, bias_c, gm, state, prev, layer, depth, n_tiles, n_sub, seq_len, n_chunk):
    assert n_sub == 1 or n_chunk == 1
    assert n_tiles % MLSTM_TILES_PER_STEP == 0
    par = MLSTM_TILES_PER_STEP
    hh, dh = MLSTM_HEADS, MLSTM_HEAD_DIM
    chunk = n_sub * seq_len
    n_slots = par * n_sub
    n_all = n_tiles * n_sub
    has_state = state is not None
    tile3 = lambda t: t.reshape(n_tiles, n_chunk * chunk, t.shape[-1])
    act_spec = lambda w: pl.BlockSpec((par, chunk, w), lambda b, c: (b, c, 0))
    st5 = lambda b, c: (layer, b, 0, 0, 0)
    st4 = lambda b, c: (layer, b, 0, 0)
    state_specs = [pl.BlockSpec((None, n_slots, hh, dh, dh), st5),
                   pl.BlockSpec((None, n_slots, hh, dh), st4),
                   pl.BlockSpec((None, n_slots, 1, hh), st4),
                   pl.BlockSpec((None, n_slots, CONV_HIST, 2 * MLSTM_WIDTH), st4)]
    in_specs = [act_spec(2 * MLSTM_WIDTH), act_spec(MLSTM_WIDTH), act_spec(LANES),
                _layer_spec((CONV_WIDTH, 2 * MLSTM_WIDTH), layer), _layer_spec((1, 2 * MLSTM_WIDTH), layer),
                _layer_spec((1, LANES), layer), _layer_spec((1, MLSTM_WIDTH), layer)]
    args = [tile3(qk), tile3(v), tile3(gc), cw, cb, bias_c, gm]
    if has_state:
        in_specs += state_specs
        args += list(state)
    aliases = {len(args) + k: 1 + k for k in range(len(prev))}
    in_specs += [_ANY] * len(prev)
    args += list(prev)
    out_shape = [jax.ShapeDtypeStruct((n_tiles, n_chunk * chunk, MLSTM_WIDTH), F32),
                 jax.ShapeDtypeStruct((depth, n_all, hh, dh, dh), F32),
                 jax.ShapeDtypeStruct((depth, n_all, hh, dh), F32),
                 jax.ShapeDtypeStruct((depth, n_all, 1, hh), F32),
                 jax.ShapeDtypeStruct((depth, n_all, CONV_HIST, 2 * MLSTM_WIDTH), F32)]
    region = SUBLANES + -(-seq_len // SUBLANES) * SUBLANES
    scratch = [pltpu.VMEM((n_slots * region, 2 * MLSTM_WIDTH), F32)]
    if n_sub > 1:
        scratch.append(pltpu.VMEM((par, chunk, 2 * MLSTM_WIDTH), F32))
    return pl.pallas_call(
        functools.partial(_mlstm_body, n_sub=n_sub, seq_len=seq_len, has_state=has_state,
                          n_prev=len(prev), single_chunk=n_chunk == 1),
        grid=(n_tiles // par, n_chunk),
        in_specs=in_specs,
        out_specs=[act_spec(MLSTM_WIDTH)] + state_specs,
        out_shape=out_shape,
        scratch_shapes=scratch,
        input_output_aliases=aliases,
        compiler_params=_params(("parallel", "arbitrary")),
        name="mlstm",
    )(*args)


def _post_body(x_ref, y_ref, hm_ref, g_ref, wo_ref, wgs_ref, wgm_ref, wglu_ref, wsu_ref, wmu_ref,
               wout_ref, o_ref):
    x = x_ref[...]
    h = _rms(x, g_ref[...]).astype(BF16)
    ys = jax.nn.gelu(y_ref[...])
    ys = ys * jax.nn.sigmoid(_dot(ys.astype(BF16), wglu_ref[...]))
    hm = hm_ref[...] * jax.nn.sigmoid(_dot(h, wo_ref[...]))
    merged = (jax.nn.sigmoid(_dot(h, wgs_ref[...])) * _dot(ys.astype(BF16), wsu_ref[...])
              + jax.nn.sigmoid(_dot(h, wgm_ref[...])) * _dot(hm.astype(BF16), wmu_ref[...]))
    o_ref[...] = x + _dot(merged.astype(BF16), wout_ref[...])


def _post(x, y, hm, g, wo, wgs, wgm, wglu, wsu, wmu, wout, layer):
    rows = x.shape[0]
    tm = min(ROW_TILE, rows)
    rb = lambda i: (i, 0)
    sq = _layer_spec((D_MODEL, D_MODEL), layer)
    return pl.pallas_call(
        _post_body,
        grid=(rows // tm,),
        in_specs=[pl.BlockSpec((tm, D_MODEL), rb), pl.BlockSpec((tm, S5_WIDTH), rb),
                  pl.BlockSpec((tm, MLSTM_WIDTH), rb),
                  _layer_spec((1, D_MODEL), layer), sq, sq, sq,
                  _layer_spec((S5_WIDTH, S5_WIDTH), layer), _layer_spec((S5_WIDTH, D_MODEL), layer), sq, sq],
        out_specs=pl.BlockSpec((tm, D_MODEL), rb),
        out_shape=jax.ShapeDtypeStruct((rows, D_MODEL), F32),
        compiler_params=_params(("parallel",)),
        name="post",
    )(x, y, hm, g, wo, wgs, wgm, wglu, wsu, wmu, wout)


def _block_diag(t):
    depth, nb, gpb, a, c = t.shape
    eye = jnp.eye(gpb, dtype=t.dtype)
    return jnp.einsum("ljgac,gk->ljgakc", t, eye).reshape(depth, nb, gpb * a, gpb * c)


def kernel(x_prompt, x_sample, state_s5_re, state_s5_im, state_mlstm_c, state_mlstm_n, state_mlstm_m, state_conv, p_prompt, p_sample, g_ffn1, w1_gate, w1_up, w1_down, g_mix, w_in, s5_lambda_re, s5_lambda_im, s5_log_dt, s5_b_re, s5_b_im, s5_c_re, s5_c_im, s5_d, s5_w_glu, w_s5_up, conv_w, conv_b, b_igate, b_fgate, g_mhead, w_m_up, w_out, g_ffn2, w2_gate, w2_up, w2_down, g_ple, w_ple, w_ple_gate, g_final):
    depth = w_in.shape[0]
    bp, sp, _ = x_prompt.shape
    bs, ss, _ = x_sample.shape
    hh = MLSTM_HEADS
    gpb = S5_GROUPS // S5_BLOCKS

    bf = lambda w: w.astype(BF16)
    cuts = [0, S5_WIDTH, S5_WIDTH + 2 * MLSTM_WIDTH, S5_WIDTH + 3 * MLSTM_WIDTH, S5_WIDTH + 4 * MLSTM_WIDTH]
    c_gate = cuts[4] + 2 * MLSTM_HEADS
    w_u = bf(w_in[:, :, cuts[0]:cuts[1]])
    w_qk = bf(w_in[:, :, cuts[1]:cuts[2]])
    w_v = bf(w_in[:, :, cuts[2]:cuts[3]])
    w_o = bf(w_in[:, :, cuts[3]:cuts[4]])
    w_gc = bf(jnp.pad(w_in[:, :, cuts[4]:c_gate], ((0, 0), (0, 0), (0, LANES - 2 * MLSTM_HEADS))))
    w_gs = bf(w_in[:, :, c_gate:c_gate + D_MODEL])
    w_gm = bf(w_in[:, :, c_gate + D_MODEL:])
    w1g, w1u, w1d = bf(w1_gate), bf(w1_up), bf(w1_down)
    w2g, w2u, w2d = bf(w2_gate), bf(w2_up), bf(w2_down)
    wglu, wsu, wmu, wout = bf(s5_w_glu), bf(w_s5_up), bf(w_m_up), bf(w_out)
    wple, wpg = bf(w_ple), bf(w_ple_gate)
    row = lambda g: g.reshape(depth, 1, -1)
    gf1, gmx, gf2, gpl, gmh, s5d, cvb = (row(g_ffn1), row(g_mix), row(g_ffn2), row(g_ple),
                                          row(g_mhead), row(s5_d), row(conv_b))
    gfin = g_final.reshape(1, D_MODEL)
    bias_c = jnp.pad(jnp.concatenate([b_igate, b_fgate], axis=1),
                     ((0, 0), (0, LANES - 2 * MLSTM_HEADS))).reshape(depth, 1, LANES)

    lanes3 = lambda t: t.reshape(depth, 1, S5_LANES)
    ldt = jnp.broadcast_to(s5_log_dt[:, :, None], (depth, S5_GROUPS, S5_STATE))
    to_cols = lambda t: jnp.transpose(t, (0, 3, 1, 2)).reshape(depth, S5_GROUP, S5_LANES)
    a_re, a_im, bb_re, bb_im = _s5_params(lanes3(s5_lambda_re), lanes3(s5_lambda_im), lanes3(ldt),
                                          to_cols(s5_b_re), to_cols(s5_b_im))
    to_blk = lambda t: jnp.transpose(t.reshape(depth, S5_GROUP, S5_BLOCKS, gpb, S5_STATE), (0, 2, 3, 1, 4))
    bmat = bf(jnp.concatenate([_block_diag(to_blk(bb_re)), _block_diag(to_blk(bb_im))], axis=-1))
    c_blk = lambda t: jnp.transpose(t.reshape(depth, S5_BLOCKS, gpb, S5_GROUP, S5_STATE), (0, 1, 2, 4, 3))
    cmat_re = bf(_block_diag(c_blk(s5_c_re)))
    cmat_im = bf(_block_diag(c_blk(s5_c_im)))
    s5_time_p = S5_ROWS // bp
    perm_p = _time_major_perm(bp, s5_time_p)
    perm_s = _time_major_perm(bs, ss)

    xp = x_prompt.reshape(bp * sp, D_MODEL)
    xs = x_sample.reshape(bs * ss, D_MODEL)
    pp = p_prompt.reshape(depth, bp * sp, D_PLE)
    ps = p_sample.reshape(depth, bs * ss, D_PLE)
    to_blocks = lambda t: jnp.swapaxes(t.reshape(depth, -1, S5_BLOCKS, S5_BLOCK_ST), 1, 2)
    from_blocks = lambda t: jnp.swapaxes(t, 1, 2).reshape(depth, -1, S5_GROUPS, S5_STATE)
    a_re = a_re.reshape(depth, S5_BLOCKS, 1, S5_BLOCK_ST)
    a_im = a_im.reshape(depth, S5_BLOCKS, 1, S5_BLOCK_ST)
    s5_state = (to_blocks(state_s5_re), to_blocks(state_s5_im))
    lstm_state = (state_mlstm_c, state_mlstm_n, state_mlstm_m.reshape(depth, bs, 1, hh), state_conv)

    s5_out_p, s5_out_s, lstm_out_p, lstm_out_s = [], [], [], []
    for i in range(depth):
        final = i == depth - 1
        ffn1_w = (gf1, w1g, w1u, w1d)
        proj_w = (gmx, w_u, w_qk, w_v, w_gc)
        s5_w = (a_re, a_im, bmat, cmat_re, cmat_im, s5d)
        lstm_w = (conv_w, cvb, bias_c, gmh)
        mix_w = (gmx, w_o, w_gs, w_gm, wglu, wsu, wmu, wout)
        ple_w = (gf2, w2g, w2u, w2d, gpl, wple, wpg, gfin)

        xp = _ffn(xp, *ffn1_w, layer=i)
        u, qk, v, gc = _inproj(xp, *proj_w, layer=i)
        y, *s5_out_p = _s5(u, perm_p, perm_p.T, *s5_w, None, s5_out_p,
                           layer=i, depth=depth, n_batch=bp, n_time=s5_time_p)
        hm, *lstm_out_p = _mlstm(qk, v, gc, *lstm_w, None, lstm_out_p, layer=i, depth=depth,
                                 n_tiles=bp, n_sub=1, seq_len=MLSTM_CHUNK, n_chunk=sp // MLSTM_CHUNK)
        xp = _post(xp, y.reshape(bp * sp, S5_WIDTH), hm.reshape(bp * sp, MLSTM_WIDTH), *mix_w, layer=i)
        xp = _ffn_ple(xp, pp, *ple_w, layer=i, final=final)

        xs = _ffn(xs, *ffn1_w, layer=i)
        u, qk, v, gc = _inproj(xs, *proj_w, layer=i)
        y, *s5_out_s = _s5(u, perm_s, perm_s.T, *s5_w, s5_state, s5_out_s,
                           layer=i, depth=depth, n_batch=bs, n_time=ss)
        hm, *lstm_out_s = _mlstm(qk, v, gc, *lstm_w, lstm_state, lstm_out_s, layer=i, depth=depth,
                                 n_tiles=bs // SAMPLE_SEQS_PER_TILE, n_sub=SAMPLE_SEQS_PER_TILE,
                                 seq_len=ss, n_chunk=1)
        xs = _post(xs, y.reshape(bs * ss, S5_WIDTH), hm.reshape(bs * ss, MLSTM_WIDTH), *mix_w, layer=i)
        xs = _ffn_ple(xs, ps, *ple_w, layer=i, final=final)

    def states(n, s5_out, lstm_out):
        c_n, n_n, m_n, conv_n = lstm_out
        return (from_blocks(s5_out[0]), from_blocks(s5_out[1]), c_n, n_n, m_n.reshape(depth, n, hh), conv_n)

    return ((xp.reshape(bp, sp, D_MODEL), xs.reshape(bs, ss, D_MODEL))
            + states(bp, s5_out_p, lstm_out_p) + states(bs, s5_out_s, lstm_out_s))
```

```python
import functools
import itertools

import jax
import jax.numpy as jnp
from jax import lax
from jax.experimental import pallas as pl
from jax.experimental.pallas import tpu as pltpu

D_MODEL = 1024
D_PLE = 256
D_FF = 2048
S5_WIDTH = 512
S5_GROUP = 16
S5_GROUPS = S5_WIDTH // S5_GROUP
S5_STATE = 64
S5_LANES = S5_GROUPS * S5_STATE
MLSTM_WIDTH = 1024
MLSTM_HEADS = 4
MLSTM_HEAD_DIM = MLSTM_WIDTH // MLSTM_HEADS
MLSTM_CHUNK = 128
CONV_WIDTH = 4
CONV_HIST = CONV_WIDTH - 1
EPS = 1e-6

F32 = jnp.float32
BF16 = jnp.bfloat16
HIGHEST = lax.Precision.HIGHEST

SUBLANES = 8
LANES = 128
VMEM_LIMIT_BYTES = 56 * 1024 * 1024

ROW_TILE = 512
S5_ROWS = 512
S5_BLOCKS = 4
S5_BLOCK_IN = S5_WIDTH // S5_BLOCKS
S5_BLOCK_ST = S5_LANES // S5_BLOCKS
SAMPLE_SEQS_PER_TILE = 2
PROMPT_TILES_PER_STEP = 2
SAMPLE_TILES_PER_STEP = 4


def _params(sem):
    return pltpu.CompilerParams(dimension_semantics=sem, vmem_limit_bytes=VMEM_LIMIT_BYTES)


def _layer_spec(shape, layer):
    nd = len(shape)
    return pl.BlockSpec((None,) + tuple(shape), lambda *_: (layer,) + (0,) * nd,
                        pipeline_mode=pl.Buffered(1))


def _const_spec(shape):
    nd = len(shape)
    return pl.BlockSpec(tuple(shape), lambda *_: (0,) * nd, pipeline_mode=pl.Buffered(1))


_ANY = pl.BlockSpec(memory_space=pl.ANY)


def _rms(x, g):
    return x * lax.rsqrt(jnp.mean(x * x, axis=-1, keepdims=True) + EPS) * g


def _dot(a, b):
    return jnp.dot(a, b, preferred_element_type=F32)


def _dot_nt(a, b, **kw):
    return lax.dot_general(a, b, (((1,), (1,)), ((), ())), preferred_element_type=F32, **kw)


def _dot_tn(a, b):
    return lax.dot_general(a, b, (((0,), (0,)), ((), ())), preferred_element_type=F32)


def _ffn_body(x_ref, g_ref, wg_ref, wu_ref, wd_ref, o_ref):
    x = x_ref[...]
    h = _rms(x, g_ref[...]).astype(BF16)
    z = (jax.nn.silu(_dot(h, wg_ref[...])) * _dot(h, wu_ref[...])).astype(BF16)
    o_ref[...] = x + 0.5 * _dot(z, wd_ref[...])


def _ffn_ple_body(x_ref, p_ref, g_ref, wg_ref, wu_ref, wd_ref, gp_ref, wp_ref, wpg_ref, gf_ref, o_ref,
                  *, final):
    x = x_ref[...]
    h = _rms(x, g_ref[...]).astype(BF16)
    z = (jax.nn.silu(_dot(h, wg_ref[...])) * _dot(h, wu_ref[...])).astype(BF16)
    x = x + 0.5 * _dot(z, wd_ref[...])
    gate = jax.nn.sigmoid(_dot(_rms(x, gp_ref[...]).astype(BF16), wpg_ref[...]))
    x = x + _dot(p_ref[...].astype(BF16), wp_ref[...]) * gate
    if final:
        x = _rms(x, gf_ref[...])
    o_ref[...] = x


def _ffn(x, g, wg, wu, wd, layer):
    rows = x.shape[0]
    tm = min(ROW_TILE, rows)
    row_spec = pl.BlockSpec((tm, D_MODEL), lambda i: (i, 0))
    return pl.pallas_call(
        _ffn_body,
        grid=(rows // tm,),
        in_specs=[row_spec, _layer_spec((1, D_MODEL), layer), _layer_spec((D_MODEL, D_FF), layer),
                  _layer_spec((D_MODEL, D_FF), layer), _layer_spec((D_FF, D_MODEL), layer)],
        out_specs=row_spec,
        out_shape=jax.ShapeDtypeStruct((rows, D_MODEL), F32),
        compiler_params=_params(("parallel",)),
        name="ffn",
    )(x, g, wg, wu, wd)


def _ffn_ple(x, p, g, wg, wu, wd, gp, wp, wpg, gf, layer, final):
    rows = x.shape[0]
    tm = min(ROW_TILE, rows)
    row_spec = pl.BlockSpec((tm, D_MODEL), lambda i: (i, 0))
    return pl.pallas_call(
        functools.partial(_ffn_ple_body, final=final),
        grid=(rows // tm,),
        in_specs=[row_spec, pl.BlockSpec((None, tm, D_PLE), lambda i: (layer, i, 0)),
                  _layer_spec((1, D_MODEL), layer), _layer_spec((D_MODEL, D_FF), layer),
                  _layer_spec((D_MODEL, D_FF), layer), _layer_spec((D_FF, D_MODEL), layer),
                  _layer_spec((1, D_MODEL), layer), _layer_spec((D_PLE, D_MODEL), layer),
                  _layer_spec((D_MODEL, D_MODEL), layer), _const_spec((1, D_MODEL))],
        out_specs=row_spec,
        out_shape=jax.ShapeDtypeStruct((rows, D_MODEL), F32),
        compiler_params=_params(("parallel",)),
        name="ffn_ple",
    )(x, p, g, wg, wu, wd, gp, wp, wpg, gf)


def _inproj_body(x_ref, g_ref, wu_ref, wqk_ref, wv_ref, wgc_ref, u_ref, qk_ref, v_ref, gc_ref):
    h = _rms(x_ref[...], g_ref[...]).astype(BF16)
    u_ref[...] = _dot(h, wu_ref[...])
    qk_ref[...] = _dot(h, wqk_ref[...])
    v_ref[...] = _dot(h, wv_ref[...])
    gc_ref[...] = _dot(h, wgc_ref[...])


def _inproj(x, g, wu, wqk, wv, wgc, layer):
    rows = x.shape[0]
    tm = min(ROW_TILE, rows)
    rb = lambda i: (i, 0)
    widths = (S5_WIDTH, 2 * MLSTM_WIDTH, MLSTM_WIDTH, LANES)
    return pl.pallas_call(
        _inproj_body,
        grid=(rows // tm,),
        in_specs=[pl.BlockSpec((tm, D_MODEL), rb), _layer_spec((1, D_MODEL), layer)]
        + [_layer_spec((D_MODEL, w), layer) for w in widths],
        out_specs=[pl.BlockSpec((tm, w), rb) for w in widths],
        out_shape=[jax.ShapeDtypeStruct((rows, w), F32) for w in widths],
        compiler_params=_params(("parallel",)),
        name="inproj",
    )(x, g, wu, wqk, wv, wgc)


def _s5_param_body(lre_ref, lim_ref, ldt_ref, bre_ref, bim_ref, are_ref, aim_ref, bbre_ref, bbim_ref):
    lre = lre_ref[0]
    lim = lim_ref[0]
    dt = jnp.exp(ldt_ref[0])
    mag = jnp.exp(lre * dt)
    a_re = mag * jnp.cos(lim * dt)
    a_im = mag * jnp.sin(lim * dt)
    den = lre * lre + lim * lim
    pr = a_re - 1.0
    w_re = (pr * lre + a_im * lim) / den
    w_im = (a_im * lre - pr * lim) / den
    are_ref[0] = a_re
    aim_ref[0] = a_im
    bbre_ref[0] = w_re * bre_ref[0] - w_im * bim_ref[0]
    bbim_ref[0] = w_re * bim_ref[0] + w_im * bre_ref[0]


def _s5_params(lre, lim, ldt, bre, bim):
    depth = lre.shape[0]
    vec = pl.BlockSpec((1, 1, S5_LANES), lambda i: (i, 0, 0))
    mat = pl.BlockSpec((1, S5_GROUP, S5_LANES), lambda i: (i, 0, 0))
    return pl.pallas_call(
        _s5_param_body,
        grid=(depth,),
        in_specs=[vec, vec, vec, mat, mat],
        out_specs=[vec, vec, mat, mat],
        out_shape=[jax.ShapeDtypeStruct((depth, 1, S5_LANES), F32)] * 2
        + [jax.ShapeDtypeStruct((depth, S5_GROUP, S5_LANES), F32)] * 2,
        compiler_params=_params(("parallel",)),
        name="s5_params",
    )(lre, lim, ldt, bre, bim)


def _s5_body(*refs, n_batch, n_time, has_state, n_prev):
    weights = refs[:9]
    refs = refs[9:]
    h0 = None
    if has_state:
        h0 = refs[:2]
        refs = refs[2:]
    outs = refs[n_prev:n_prev + 3]
    _s5_init(outs[1], outs[2], h0)
    _s5_chunk(*weights, *outs, *refs[n_prev + 3:], n_batch=n_batch, n_time=n_time)


def _s5_init(sr_ref, si_ref, h0):
    @pl.when(pl.program_id(0) == 0)
    def _():
        if h0 is not None:
            sr_ref[...] = h0[0][...]
            si_ref[...] = h0[1][...]
        else:
            sr_ref[...] = jnp.zeros_like(sr_ref)
            si_ref[...] = jnp.zeros_like(si_ref)


def _s5_chunk(u_ref, perm_ref, permt_ref, are_ref, aim_ref, bm_ref, cre_ref, cim_ref, d_ref,
              y_ref, sr_ref, si_ref, ub_s, bur_s, bui_s, hr_s, hi_s, y_s, *, n_batch, n_time):
    rows = n_batch * n_time
    n_bt = n_batch // SUBLANES

    u = u_ref[...].reshape(rows, S5_WIDTH)
    ub = _dot(perm_ref[...], u.astype(BF16)).astype(BF16)
    for j in range(S5_BLOCKS):
        ub_s[j] = ub[:, j * S5_BLOCK_IN:(j + 1) * S5_BLOCK_IN]

    def project_in(j):
        r = _dot(ub_s[j], bm_ref[j])
        bur_s[j] = r[:, :S5_BLOCK_ST]
        bui_s[j] = r[:, S5_BLOCK_ST:]

    def scan(j):
        a_re = jnp.broadcast_to(are_ref[j], (SUBLANES, S5_BLOCK_ST))
        a_im = jnp.broadcast_to(aim_ref[j], (SUBLANES, S5_BLOCK_ST))

        def advance(h, row):
            h_re, h_im = h
            tile = slice(row, row + SUBLANES)
            return (a_re * h_re - a_im * h_im + bur_s[j, tile, :],
                    a_re * h_im + a_im * h_re + bui_s[j, tile, :])

        def emit(row, first, second):
            pair = slice(row, row + 2 * SUBLANES)
            hr_s[j, pair, :] = jnp.concatenate([first[0], second[0]], axis=0).astype(BF16)
            hi_s[j, pair, :] = jnp.concatenate([first[1], second[1]], axis=0).astype(BF16)

        if n_bt == 1:
            h = (sr_ref[j], si_ref[j])
            for t in range(0, n_time, 2):
                h1 = advance(h, t * SUBLANES)
                h = advance(h1, (t + 1) * SUBLANES)
                emit(t * SUBLANES, h1, h)
            sr_ref[j], si_ref[j] = h
        else:
            for bt in range(0, n_bt, 2):
                rows_a = slice(bt * SUBLANES, (bt + 1) * SUBLANES)
                rows_b = slice((bt + 1) * SUBLANES, (bt + 2) * SUBLANES)
                ha = (sr_ref[j, rows_a, :], si_ref[j, rows_a, :])
                hb = (sr_ref[j, rows_b, :], si_ref[j, rows_b, :])
                for t in range(n_time):
                    row = t * n_batch + bt * SUBLANES
                    ha = advance(ha, row)
                    hb = advance(hb, row + SUBLANES)
                    emit(row, ha, hb)
                sr_ref[j, rows_a, :], si_ref[j, rows_a, :] = ha
                sr_ref[j, rows_b, :], si_ref[j, rows_b, :] = hb

    def project_out(j):
        y_s[j] = _dot(hr_s[j], cre_ref[j]) - _dot(hi_s[j], cim_ref[j])

    project_in(0)

    def pipelined(j, carry):
        project_in(j + 1)
        scan(j)
        project_out(j)
        return carry

    lax.fori_loop(0, S5_BLOCKS - 1, pipelined, 0)
    scan(S5_BLOCKS - 1)
    project_out(S5_BLOCKS - 1)
    y = jnp.concatenate([y_s[j] for j in range(S5_BLOCKS)], axis=1)
    y_hi = y.astype(BF16)
    y_lo = (y - y_hi.astype(F32)).astype(BF16)
    permt = permt_ref[...]
    y = _dot(permt, y_hi) + _dot(permt, y_lo) + d_ref[...] * u
    y_ref[...] = y.reshape(y_ref.shape)


def _s5(u, perm, permt, a_re, a_im, bm, cre, cim, d, h0, prev, layer, depth, n_batch, n_time):
    seq = u.shape[0] // n_batch
    rows = n_batch * n_time
    has_state = h0 is not None
    assert n_batch % (2 * SUBLANES) == 0 or (n_batch == SUBLANES and n_time % 2 == 0)
    lead, blk_rows = (n_batch, n_time) if seq > n_time else (1, rows)
    u = u.reshape(lead, u.shape[0] // lead, S5_WIDTH)
    u_spec = pl.BlockSpec((lead, blk_rows, S5_WIDTH), lambda c: (0, c, 0))
    state_spec = pl.BlockSpec((None, S5_BLOCKS, n_batch, S5_BLOCK_ST), lambda c: (layer, 0, 0, 0))
    state_shape = jax.ShapeDtypeStruct((depth, S5_BLOCKS, n_batch, S5_BLOCK_ST), F32)
    in_specs = [u_spec,
                _const_spec((rows, rows)), _const_spec((rows, rows)),
                _layer_spec((S5_BLOCKS, 1, S5_BLOCK_ST), layer), _layer_spec((S5_BLOCKS, 1, S5_BLOCK_ST), layer),
                _layer_spec((S5_BLOCKS, S5_BLOCK_IN, 2 * S5_BLOCK_ST), layer),
                _layer_spec((S5_BLOCKS, S5_BLOCK_ST, S5_BLOCK_IN), layer),
                _layer_spec((S5_BLOCKS, S5_BLOCK_ST, S5_BLOCK_IN), layer),
                _layer_spec((1, S5_WIDTH), layer)]
    args = [u, perm, permt, a_re, a_im, bm, cre, cim, d]
    if has_state:
        in_specs += [state_spec, state_spec]
        args += list(h0)
    aliases = {len(args) + k: 1 + k for k in range(len(prev))}
    in_specs += [_ANY] * len(prev)
    args += list(prev)
    return pl.pallas_call(
        functools.partial(_s5_body, n_batch=n_batch, n_time=n_time, has_state=has_state, n_prev=len(prev)),
        grid=(seq // n_time,),
        in_specs=in_specs,
        out_specs=[u_spec, state_spec, state_spec],
        out_shape=[jax.ShapeDtypeStruct(u.shape, F32), state_shape, state_shape],
        scratch_shapes=[pltpu.VMEM((S5_BLOCKS, rows, S5_BLOCK_IN), BF16),
                        pltpu.VMEM((S5_BLOCKS, rows, S5_BLOCK_ST), F32),
                        pltpu.VMEM((S5_BLOCKS, rows, S5_BLOCK_ST), F32),
                        pltpu.VMEM((S5_BLOCKS, rows, S5_BLOCK_ST), BF16),
                        pltpu.VMEM((S5_BLOCKS, rows, S5_BLOCK_ST), BF16),
                        pltpu.VMEM((S5_BLOCKS, rows, S5_BLOCK_IN), F32)],
        input_output_aliases=aliases,
        compiler_params=_params(("arbitrary",)),
        name="s5",
    )(*args)


def _time_major_perm(n_batch, n_time):
    r = jnp.arange(n_batch * n_time)
    src = (r % n_batch) * n_time + r // n_batch
    return (src[:, None] == r[None, :]).astype(BF16)


def _mlstm_tile(g, qk_ref, v_ref, gc_ref, cw_ref, cb_ref, bc_ref, gm_ref,
                c_in, n_in, m_in, hm_ref, c_ref, n_ref, m_ref, conv_ref, xp_s, qk_s,
                *, n_sub, seq_len, carry_hist):
    chunk = n_sub * seq_len
    region = SUBLANES + -(-seq_len // SUBLANES) * SUBLANES
    hist = SUBLANES
    slots = [g * n_sub + j for j in range(n_sub)]

    for j, slot in enumerate(slots):
        base = slot * region
        x_raw = qk_ref[g, j * seq_len:(j + 1) * seq_len, :]
        xp_s[base + hist:base + hist + seq_len, :] = x_raw
        if seq_len % SUBLANES == 0:
            x3 = x_raw.reshape(seq_len // SUBLANES, SUBLANES, 2 * MLSTM_WIDTH)
            prev = xp_s[base:base + hist, :]
            sub = lax.broadcasted_iota(jnp.int32, (1, SUBLANES, 1), 1)
            acc = cb_ref[...] + x3 * cw_ref[CONV_HIST:CONV_WIDTH, :]
            for k in range(1, CONV_WIDTH):
                rot = pltpu.roll(x3, k, axis=1)
                before = jnp.concatenate([pltpu.roll(prev, k, axis=0)[None], rot[:-1]], axis=0)
                acc = acc + jnp.where(sub < k, before, rot) * cw_ref[CONV_HIST - k:CONV_WIDTH - k, :]
            acc = acc.reshape(seq_len, 2 * MLSTM_WIDTH)
        else:
            acc = jnp.broadcast_to(cb_ref[...], (seq_len, 2 * MLSTM_WIDTH))
            for i in range(CONV_WIDTH):
                off = base + hist - CONV_HIST + i
                acc = acc + xp_s[off:off + seq_len, :] * cw_ref[i:i + 1, :]
        new_hist = xp_s[base + hist + seq_len - CONV_HIST:base + hist + seq_len, :]
        conv_ref[slot] = new_hist
        if carry_hist:
            xp_s[base + hist - CONV_HIST:base + hist, :] = new_hist
        if n_sub > 1:
            qk_s[g, j * seq_len:(j + 1) * seq_len, :] = acc
        yield
    qk = jax.nn.silu(qk_s[g] if n_sub > 1 else acc)
    q_all = qk[:, :MLSTM_WIDTH] * (MLSTM_HEAD_DIM ** -0.5)
    k_all = qk[:, MLSTM_WIDTH:]
    v_all = v_ref[g]
    yield

    def seq_of(idx):
        s = jnp.zeros(idx.shape, F32)
        for j in range(1, n_sub):
            s = s + jnp.where(idx >= j * seq_len, 1.0, 0.0)
        return s

    row_seq = seq_of(lax.broadcasted_iota(jnp.int32, (chunk, 1), 0))
    col_seq = seq_of(lax.broadcasted_iota(jnp.int32, (1, chunk), 1))
    t_id = lax.broadcasted_iota(jnp.int32, (chunk, chunk), 0)
    s_id = lax.broadcasted_iota(jnp.int32, (chunk, chunk), 1)
    same = jnp.where(row_seq == col_seq, 1.0, 0.0) if n_sub > 1 else jnp.ones((chunk, chunk), F32)
    tril = jnp.where(s_id <= t_id, same, 0.0)
    triu = jnp.where(t_id <= s_id, same, 0.0)
    causal = tril > 0.5
    pick = (lax.broadcasted_iota(jnp.int32, (SUBLANES, LANES), 0)
            == lax.broadcasted_iota(jnp.int32, (SUBLANES, LANES), 1)).astype(F32)

    li_col = gc_ref[g] + bc_ref[...]
    li_row = _dot_nt(pick, li_col, precision=HIGHEST)
    lf_col = jax.nn.log_sigmoid(li_col)
    yield
    lf_row = jax.nn.log_sigmoid(li_row)
    b_col = jnp.dot(tril, lf_col, preferred_element_type=F32, precision=HIGHEST)
    b_row = jnp.dot(lf_row, triu, preferred_element_type=F32, precision=HIGHEST)
    yield

    def per_row(vals):
        out = vals[0]
        for j in range(1, n_sub):
            out = jnp.where(row_seq == j, vals[j], out)
        return out

    m_new_parts = [[] for _ in range(n_sub)]
    for h in range(MLSTM_HEADS):
        hs = slice(h * MLSTM_HEAD_DIM, (h + 1) * MLSTM_HEAD_DIM)
        bc = b_col[:, MLSTM_HEADS + h:MLSTM_HEADS + h + 1]
        br = b_row[MLSTM_HEADS + h:MLSTM_HEADS + h + 1, :]
        lic = li_col[:, h:h + 1]
        lir = li_row[h:h + 1, :]
        m_prev = [m_in[s][:, h:h + 1] for s in slots]
        c_old = [c_in[s, h] for s in slots]
        n_old = [n_in[s, h:h + 1, :] for s in slots]
        qf = q_all[:, hs]
        kf = k_all[:, hs]
        qb = qf.astype(BF16)
        vb = v_all[:, hs].astype(BF16)

        dmat = jnp.where(causal, bc - br + lir, -jnp.inf)
        inter = bc + per_row(m_prev)
        m_t = jnp.maximum(inter, jnp.max(dmat, axis=-1, keepdims=True))
        s_qk = _dot_nt(qb, kf.astype(BF16))
        q_c = per_row([_dot(qb, c.astype(BF16)) for c in c_old])
        yield
        w = jnp.exp(dmat - m_t)
        scores = s_qk * w
        a = jnp.exp(inter - m_t)
        num = _dot(scores.astype(BF16), vb) + a * q_c
        nq = (jnp.sum(scores, axis=-1, keepdims=True)
              + a * jnp.sum(qf * per_row(n_old), axis=-1, keepdims=True))
        yield
        hout = num / jnp.maximum(jnp.abs(nq), jnp.exp(-m_t))
        hout = hout * lax.rsqrt(jnp.mean(hout * hout, axis=-1, keepdims=True) + EPS)
        hm_ref[g, :, hs] = hout * gm_ref[:, hs]
        yield

        for j, slot in enumerate(slots):
            last = (j + 1) * seq_len - 1
            b_last = br[:, last:last + 1]
            g_row = b_last - br + lir
            if n_sub > 1:
                g_row = jnp.where(col_seq == j, g_row, -jnp.inf)
            m_new = jnp.maximum(b_last + m_prev[j], jnp.max(g_row, axis=-1, keepdims=True))
            wk = jnp.exp(b_last - bc + lic - m_new)
            if n_sub > 1:
                wk = jnp.where(row_seq == j, wk, 0.0)
            decay = jnp.exp(b_last + m_prev[j] - m_new)
            kw = kf * wk
            c_ref[slot, h] = decay * c_old[j] + _dot_tn(kw.astype(BF16), vb)
            n_ref[slot, h:h + 1, :] = decay * n_old[j] + jnp.sum(kw, axis=0, keepdims=True)
            m_new_parts[j].append(m_new)
            yield
    for j, slot in enumerate(slots):
        m_ref[slot] = jnp.concatenate(m_new_parts[j], axis=1)


def _mlstm_body(*refs, n_sub, seq_len, has_state, n_prev, single_chunk, par):
    qk_ref, v_ref, gc_ref, cw_ref, cb_ref, bc_ref, gm_ref = refs[:7]
    refs = refs[7:]
    c0_ref = n0_ref = m0_ref = conv0_ref = None
    if has_state:
        c0_ref, n0_ref, m0_ref, conv0_ref = refs[:4]
        refs = refs[4:]
    hm_ref, c_ref, n_ref, m_ref, conv_ref = refs[n_prev:n_prev + 5]
    scratch = refs[n_prev + 5:]
    xp_s = scratch[0]
    qk_s = scratch[1] if n_sub > 1 else None

    from_input = has_state and single_chunk
    c_in, n_in, m_in = (c0_ref, n0_ref, m0_ref) if from_input else (c_ref, n_ref, m_ref)

    @pl.when(pl.program_id(1) == 0)
    def _():
        _mlstm_init(c_ref, n_ref, m_ref, xp_s, None if from_input else (c0_ref, n0_ref, m0_ref), conv0_ref,
                    state_from_input=from_input)

    tiles = [_mlstm_tile(g, qk_ref, v_ref, gc_ref, cw_ref, cb_ref, bc_ref, gm_ref,
                         c_in, n_in, m_in, hm_ref, c_ref, n_ref, m_ref, conv_ref, xp_s, qk_s,
                         n_sub=n_sub, seq_len=seq_len, carry_hist=not single_chunk) for g in range(par)]
    order = itertools.zip_longest(*tiles) if single_chunk else itertools.chain(*tiles)
    for _ in order:
        pass


def _mlstm_init(c_ref, n_ref, m_ref, xp_s, state0, conv0_ref, *, state_from_input):
    n_slots = c_ref.shape[0]
    region = xp_s.shape[0] // n_slots
    for slot in range(n_slots):
        xp_s[slot * region:slot * region + SUBLANES, :] = jnp.zeros((SUBLANES, 2 * MLSTM_WIDTH), F32)
        if conv0_ref is not None:
            xp_s[slot * region + SUBLANES - CONV_HIST:slot * region + SUBLANES, :] = conv0_ref[slot]
    if state_from_input:
        return
    for ref, ref0 in zip((c_ref, n_ref, m_ref), state0):
        ref[...] = jnp.zeros_like(ref) if ref0 is None else ref0[...]


def _mlstm(qk, v, gc, cw, cb, bias_c, gm, state, prev, layer, depth, n_tiles, n_sub, seq_len, n_chunk, par):
    assert n_sub == 1 or n_chunk == 1
    assert n_tiles % par == 0
    hh, dh = MLSTM_HEADS, MLSTM_HEAD_DIM
    chunk = n_sub * seq_len
    n_slots = par * n_sub
    n_all = n_tiles * n_sub
    has_state = state is not None
    tile3 = lambda t: t.reshape(n_tiles, n_chunk * chunk, t.shape[-1])
    act_spec = lambda w: pl.BlockSpec((par, chunk, w), lambda b, c: (b, c, 0))
    st5 = lambda b, c: (layer, b, 0, 0, 0)
    st4 = lambda b, c: (layer, b, 0, 0)
    state_specs = [pl.BlockSpec((None, n_slots, hh, dh, dh), st5),
                   pl.BlockSpec((None, n_slots, hh, dh), st4),
                   pl.BlockSpec((None, n_slots, 1, hh), st4),
                   pl.BlockSpec((None, n_slots, CONV_HIST, 2 * MLSTM_WIDTH), st4)]
    in_specs = [act_spec(2 * MLSTM_WIDTH), act_spec(MLSTM_WIDTH), act_spec(LANES),
                _layer_spec((CONV_WIDTH, 2 * MLSTM_WIDTH), layer), _layer_spec((1, 2 * MLSTM_WIDTH), layer),
                _layer_spec((1, LANES), layer), _layer_spec((1, MLSTM_WIDTH), layer)]
    args = [tile3(qk), tile3(v), tile3(gc), cw, cb, bias_c, gm]
    if has_state:
        in_specs += state_specs
        args += list(state)
    aliases = {len(args) + k: 1 + k for k in range(len(prev))}
    in_specs += [_ANY] * len(prev)
    args += list(prev)
    out_shape = [jax.ShapeDtypeStruct((n_tiles, n_chunk * chunk, MLSTM_WIDTH), F32),
                 jax.ShapeDtypeStruct((depth, n_all, hh, dh, dh), F32),
                 jax.ShapeDtypeStruct((depth, n_all, hh, dh), F32),
                 jax.ShapeDtypeStruct((depth, n_all, 1, hh), F32),
                 jax.ShapeDtypeStruct((depth, n_all, CONV_HIST, 2 * MLSTM_WIDTH), F32)]
    region = SUBLANES + -(-seq_len // SUBLANES) * SUBLANES
    scratch = [pltpu.VMEM((n_slots * region, 2 * MLSTM_WIDTH), F32)]
    if n_sub > 1:
        scratch.append(pltpu.VMEM((par, chunk, 2 * MLSTM_WIDTH), F32))
    return pl.pallas_call(
        functools.partial(_mlstm_body, n_sub=n_sub, seq_len=seq_len, has_state=has_state,
                          n_prev=len(prev), single_chunk=n_chunk == 1, par=par),
        grid=(n_tiles // par, n_chunk),
        in_specs=in_specs,
        out_specs=[act_spec(MLSTM_WIDTH)] + state_specs,
        out_shape=out_shape,
        scratch_shapes=scratch,
        input_output_aliases=aliases,
        compiler_params=_params(("parallel", "arbitrary")),
        name="mlstm",
    )(*args)


def _post_body(x_ref, y_ref, hm_ref, g_ref, wo_ref, wgs_ref, wgm_ref, wglu_ref, wsu_ref, wmu_ref,
               wout_ref, o_ref):
    x = x_ref[...]
    h = _rms(x, g_ref[...]).astype(BF16)
    ys = jax.nn.gelu(y_ref[...])
    ys = ys * jax.nn.sigmoid(_dot(ys.astype(BF16), wglu_ref[...]))
    hm = hm_ref[...] * jax.nn.sigmoid(_dot(h, wo_ref[...]))
    merged = (jax.nn.sigmoid(_dot(h, wgs_ref[...])) * _dot(ys.astype(BF16), wsu_ref[...])
              + jax.nn.sigmoid(_dot(h, wgm_ref[...])) * _dot(hm.astype(BF16), wmu_ref[...]))
    o_ref[...] = x + _dot(merged.astype(BF16), wout_ref[...])


def _post(x, y, hm, g, wo, wgs, wgm, wglu, wsu, wmu, wout, layer):
    rows = x.shape[0]
    tm = min(ROW_TILE, rows)
    rb = lambda i: (i, 0)
    sq = _layer_spec((D_MODEL, D_MODEL), layer)
    return pl.pallas_call(
        _post_body,
        grid=(rows // tm,),
        in_specs=[pl.BlockSpec((tm, D_MODEL), rb), pl.BlockSpec((tm, S5_WIDTH), rb),
                  pl.BlockSpec((tm, MLSTM_WIDTH), rb),
                  _layer_spec((1, D_MODEL), layer), sq, sq, sq,
                  _layer_spec((S5_WIDTH, S5_WIDTH), layer), _layer_spec((S5_WIDTH, D_MODEL), layer), sq, sq],
        out_specs=pl.BlockSpec((tm, D_MODEL), rb),
        out_shape=jax.ShapeDtypeStruct((rows, D_MODEL), F32),
        compiler_params=_params(("parallel",)),
        name="post",
    )(x, y, hm, g, wo, wgs, wgm, wglu, wsu, wmu, wout)


def _block_diag(t):
    depth, nb, gpb, a, c = t.shape
    eye = jnp.eye(gpb, dtype=t.dtype)
    return jnp.einsum("ljgac,gk->ljgakc", t, eye).reshape(depth, nb, gpb * a, gpb * c)


def kernel(x_prompt, x_sample, state_s5_re, state_s5_im, state_mlstm_c, state_mlstm_n, state_mlstm_m, state_conv, p_prompt, p_sample, g_ffn1, w1_gate, w1_up, w1_down, g_mix, w_in, s5_lambda_re, s5_lambda_im, s5_log_dt, s5_b_re, s5_b_im, s5_c_re, s5_c_im, s5_d, s5_w_glu, w_s5_up, conv_w, conv_b, b_igate, b_fgate, g_mhead, w_m_up, w_out, g_ffn2, w2_gate, w2_up, w2_down, g_ple, w_ple, w_ple_gate, g_final):
    depth = w_in.shape[0]
    bp, sp, _ = x_prompt.shape
    bs, ss, _ = x_sample.shape
    hh = MLSTM_HEADS
    gpb = S5_GROUPS // S5_BLOCKS

    bf = lambda w: w.astype(BF16)
    cuts = [0, S5_WIDTH, S5_WIDTH + 2 * MLSTM_WIDTH, S5_WIDTH + 3 * MLSTM_WIDTH, S5_WIDTH + 4 * MLSTM_WIDTH]
    c_gate = cuts[4] + 2 * MLSTM_HEADS
    w_u = bf(w_in[:, :, cuts[0]:cuts[1]])
    w_qk = bf(w_in[:, :, cuts[1]:cuts[2]])
    w_v = bf(w_in[:, :, cuts[2]:cuts[3]])
    w_o = bf(w_in[:, :, cuts[3]:cuts[4]])
    w_gc = bf(jnp.pad(w_in[:, :, cuts[4]:c_gate], ((0, 0), (0, 0), (0, LANES - 2 * MLSTM_HEADS))))
    w_gs = bf(w_in[:, :, c_gate:c_gate + D_MODEL])
    w_gm = bf(w_in[:, :, c_gate + D_MODEL:])
    w1g, w1u, w1d = bf(w1_gate), bf(w1_up), bf(w1_down)
    w2g, w2u, w2d = bf(w2_gate), bf(w2_up), bf(w2_down)
    wglu, wsu, wmu, wout = bf(s5_w_glu), bf(w_s5_up), bf(w_m_up), bf(w_out)
    wple, wpg = bf(w_ple), bf(w_ple_gate)
    row = lambda g: g.reshape(depth, 1, -1)
    gf1, gmx, gf2, gpl, gmh, s5d, cvb = (row(g_ffn1), row(g_mix), row(g_ffn2), row(g_ple),
                                          row(g_mhead), row(s5_d), row(conv_b))
    gfin = g_final.reshape(1, D_MODEL)
    bias_c = jnp.pad(jnp.concatenate([b_igate, b_fgate], axis=1),
                     ((0, 0), (0, LANES - 2 * MLSTM_HEADS))).reshape(depth, 1, LANES)

    lanes3 = lambda t: t.reshape(depth, 1, S5_LANES)
    ldt = jnp.broadcast_to(s5_log_dt[:, :, None], (depth, S5_GROUPS, S5_STATE))
    to_cols = lambda t: jnp.transpose(t, (0, 3, 1, 2)).reshape(depth, S5_GROUP, S5_LANES)
    a_re, a_im, bb_re, bb_im = _s5_params(lanes3(s5_lambda_re), lanes3(s5_lambda_im), lanes3(ldt),
                                          to_cols(s5_b_re), to_cols(s5_b_im))
    to_blk = lambda t: jnp.transpose(t.reshape(depth, S5_GROUP, S5_BLOCKS, gpb, S5_STATE), (0, 2, 3, 1, 4))
    bmat = bf(jnp.concatenate([_block_diag(to_blk(bb_re)), _block_diag(to_blk(bb_im))], axis=-1))
    c_blk = lambda t: jnp.transpose(t.reshape(depth, S5_BLOCKS, gpb, S5_GROUP, S5_STATE), (0, 1, 2, 4, 3))
    cmat_re = bf(_block_diag(c_blk(s5_c_re)))
    cmat_im = bf(_block_diag(c_blk(s5_c_im)))
    s5_time_p = S5_ROWS // bp
    perm_p = _time_major_perm(bp, s5_time_p)
    perm_s = _time_major_perm(bs, ss)

    xp = x_prompt.reshape(bp * sp, D_MODEL)
    xs = x_sample.reshape(bs * ss, D_MODEL)
    pp = p_prompt.reshape(depth, bp * sp, D_PLE)
    ps = p_sample.reshape(depth, bs * ss, D_PLE)
    to_blocks = lambda t: jnp.swapaxes(t.reshape(depth, -1, S5_BLOCKS, S5_BLOCK_ST), 1, 2)
    from_blocks = lambda t: jnp.swapaxes(t, 1, 2).reshape(depth, -1, S5_GROUPS, S5_STATE)
    a_re = a_re.reshape(depth, S5_BLOCKS, 1, S5_BLOCK_ST)
    a_im = a_im.reshape(depth, S5_BLOCKS, 1, S5_BLOCK_ST)
    s5_state = (to_blocks(state_s5_re), to_blocks(state_s5_im))
    lstm_state = (state_mlstm_c, state_mlstm_n, state_mlstm_m.reshape(depth, bs, 1, hh), state_conv)

    s5_out_p, s5_out_s, lstm_out_p, lstm_out_s = [], [], [], []
    for i in range(depth):
        final = i == depth - 1
        ffn1_w = (gf1, w1g, w1u, w1d)
        proj_w = (gmx, w_u, w_qk, w_v, w_gc)
        s5_w = (a_re, a_im, bmat, cmat_re, cmat_im, s5d)
        lstm_w = (conv_w, cvb, bias_c, gmh)
        mix_w = (gmx, w_o, w_gs, w_gm, wglu, wsu, wmu, wout)
        ple_w = (gf2, w2g, w2u, w2d, gpl, wple, wpg, gfin)

        xp = _ffn(xp, *ffn1_w, layer=i)
        u, qk, v, gc = _inproj(xp, *proj_w, layer=i)
        y, *s5_out_p = _s5(u, perm_p, perm_p.T, *s5_w, None, s5_out_p,
                           layer=i, depth=depth, n_batch=bp, n_time=s5_time_p)
        hm, *lstm_out_p = _mlstm(qk, v, gc, *lstm_w, None, lstm_out_p, layer=i, depth=depth,
                                 n_tiles=bp, n_sub=1, seq_len=MLSTM_CHUNK, n_chunk=sp // MLSTM_CHUNK,
                                 par=PROMPT_TILES_PER_STEP)
        xp = _post(xp, y.reshape(bp * sp, S5_WIDTH), hm.reshape(bp * sp, MLSTM_WIDTH), *mix_w, layer=i)
        xp = _ffn_ple(xp, pp, *ple_w, layer=i, final=final)

        xs = _ffn(xs, *ffn1_w, layer=i)
        u, qk, v, gc = _inproj(xs, *proj_w, layer=i)
        y, *s5_out_s = _s5(u, perm_s, perm_s.T, *s5_w, s5_state, s5_out_s,
                           layer=i, depth=depth, n_batch=bs, n_time=ss)
        hm, *lstm_out_s = _mlstm(qk, v, gc, *lstm_w, lstm_state, lstm_out_s, layer=i, depth=depth,
                                 n_tiles=bs // SAMPLE_SEQS_PER_TILE, n_sub=SAMPLE_SEQS_PER_TILE,
                                 seq_len=ss, n_chunk=1, par=SAMPLE_TILES_PER_STEP)
        xs = _post(xs, y.reshape(bs * ss, S5_WIDTH), hm.reshape(bs * ss, MLSTM_WIDTH), *mix_w, layer=i)
        xs = _ffn_ple(xs, ps, *ple_w, layer=i, final=final)

    def states(n, s5_out, lstm_out):
        c_n, n_n, m_n, conv_n = lstm_out
        return (from_blocks(s5_out[0]), from_blocks(s5_out[1]), c_n, n_n, m_n.reshape(depth, n, hh), conv_n)

    return ((xp.reshape(bp, sp, D_MODEL), xs.reshape(bs, ss, D_MODEL))
            + states(bp, s5_out_p, lstm_out_p) + states(bs, s5_out_s, lstm_out_s))
```

```python
import functools
import itertools

import jax
import jax.numpy as jnp
from jax import lax
from jax.experimental import pallas as pl
from jax.experimental.pallas import tpu as pltpu

D_MODEL = 1024
D_PLE = 256
D_FF = 2048
S5_WIDTH = 512
S5_GROUP = 16
S5_GROUPS = S5_WIDTH // S5_GROUP
S5_STATE = 64
S5_LANES = S5_GROUPS * S5_STATE
MLSTM_WIDTH = 1024
MLSTM_HEADS = 4
MLSTM_HEAD_DIM = MLSTM_WIDTH // MLSTM_HEADS
MLSTM_CHUNK = 128
CONV_WIDTH = 4
CONV_HIST = CONV_WIDTH - 1
EPS = 1e-6

F32 = jnp.float32
BF16 = jnp.bfloat16
HIGHEST = lax.Precision.HIGHEST

SUBLANES = 8
LANES = 128
VMEM_LIMIT_BYTES = 56 * 1024 * 1024

ROW_TILE = 512
INPROJ_BLOCK = 512
INPROJ_BLOCKS = (S5_WIDTH + 3 * MLSTM_WIDTH) // INPROJ_BLOCK
GATE_COL = S5_WIDTH + 4 * MLSTM_WIDTH
S5_ROWS = 512
S5_BLOCKS = 4
S5_BLOCK_IN = S5_WIDTH // S5_BLOCKS
S5_BLOCK_ST = S5_LANES // S5_BLOCKS
SAMPLE_SEQS_PER_TILE = 2
PROMPT_TILES_PER_STEP = 2
SAMPLE_TILES_PER_STEP = 4


def _params(sem):
    return pltpu.CompilerParams(dimension_semantics=sem, vmem_limit_bytes=VMEM_LIMIT_BYTES)


def _layer_spec(shape, layer):
    nd = len(shape)
    return pl.BlockSpec((None,) + tuple(shape), lambda *_: (layer,) + (0,) * nd,
                        pipeline_mode=pl.Buffered(1))


def _const_spec(shape):
    nd = len(shape)
    return pl.BlockSpec(tuple(shape), lambda *_: (0,) * nd, pipeline_mode=pl.Buffered(1))


_ANY = pl.BlockSpec(memory_space=pl.ANY)


def _rms(x, g):
    return x * lax.rsqrt(jnp.mean(x * x, axis=-1, keepdims=True) + EPS) * g


def _dot(a, b):
    return jnp.dot(a, b, preferred_element_type=F32)


def _dot_nt(a, b, **kw):
    return lax.dot_general(a, b, (((1,), (1,)), ((), ())), preferred_element_type=F32, **kw)


def _dot_tn(a, b):
    return lax.dot_general(a, b, (((0,), (0,)), ((), ())), preferred_element_type=F32)


def _ffn_body(x_ref, g_ref, wg_ref, wu_ref, wd_ref, o_ref, wg_s, wu_s, wd_s):
    @pl.when(pl.program_id(0) == 0)
    def _():
        wg_s[...] = wg_ref[...].astype(BF16)
        wu_s[...] = wu_ref[...].astype(BF16)
        wd_s[...] = wd_ref[...].astype(BF16)

    x = x_ref[...]
    h = _rms(x, g_ref[...]).astype(BF16)
    z = (jax.nn.silu(_dot(h, wg_s[...])) * _dot(h, wu_s[...])).astype(BF16)
    o_ref[...] = x + 0.5 * _dot(z, wd_s[...])


def _ffn_ple_body(x_ref, p_ref, g_ref, wg_ref, wu_ref, wd_ref, gp_ref, wp_ref, wpg_ref, gf_ref, o_ref,
                  *, final):
    x = x_ref[...]
    h = _rms(x, g_ref[...]).astype(BF16)
    z = (jax.nn.silu(_dot(h, wg_ref[...])) * _dot(h, wu_ref[...])).astype(BF16)
    x = x + 0.5 * _dot(z, wd_ref[...])
    gate = jax.nn.sigmoid(_dot(_rms(x, gp_ref[...]).astype(BF16), wpg_ref[...]))
    x = x + _dot(p_ref[...].astype(BF16), wp_ref[...]) * gate
    if final:
        x = _rms(x, gf_ref[...])
    o_ref[...] = x


def _ffn(x, g, wg, wu, wd, layer):
    rows = x.shape[0]
    tm = min(ROW_TILE, rows)
    row_spec = pl.BlockSpec((tm, D_MODEL), lambda i: (i, 0))
    return pl.pallas_call(
        _ffn_body,
        grid=(rows // tm,),
        in_specs=[row_spec, _layer_spec((1, D_MODEL), layer), _layer_spec((D_MODEL, D_FF), layer),
                  _layer_spec((D_MODEL, D_FF), layer), _layer_spec((D_FF, D_MODEL), layer)],
        out_specs=row_spec,
        out_shape=jax.ShapeDtypeStruct((rows, D_MODEL), F32),
        scratch_shapes=[pltpu.VMEM((D_MODEL, D_FF), BF16), pltpu.VMEM((D_MODEL, D_FF), BF16),
                        pltpu.VMEM((D_FF, D_MODEL), BF16)],
        compiler_params=_params(("arbitrary",)),
        name="ffn",
    )(x, g, wg, wu, wd)


def _ffn_ple(x, p, g, wg, wu, wd, gp, wp, wpg, gf, layer, final):
    rows = x.shape[0]
    tm = min(ROW_TILE, rows)
    row_spec = pl.BlockSpec((tm, D_MODEL), lambda i: (i, 0))
    return pl.pallas_call(
        functools.partial(_ffn_ple_body, final=final),
        grid=(rows // tm,),
        in_specs=[row_spec, pl.BlockSpec((None, tm, D_PLE), lambda i: (layer, i, 0)),
                  _layer_spec((1, D_MODEL), layer), _layer_spec((D_MODEL, D_FF), layer),
                  _layer_spec((D_MODEL, D_FF), layer), _layer_spec((D_FF, D_MODEL), layer),
                  _layer_spec((1, D_MODEL), layer), _layer_spec((D_PLE, D_MODEL), layer),
                  _layer_spec((D_MODEL, D_MODEL), layer), _const_spec((1, D_MODEL))],
        out_specs=row_spec,
        out_shape=jax.ShapeDtypeStruct((rows, D_MODEL), F32),
        compiler_params=_params(("parallel",)),
        name="ffn_ple",
    )(x, p, g, wg, wu, wd, gp, wp, wpg, gf)


def _inproj_body(*refs):
    x_ref, g_ref = refs[:2]
    w_refs = refs[2:2 + INPROJ_BLOCKS + 1]
    u_ref, qk_ref, v_ref, gc_ref, w_s = refs[2 + INPROJ_BLOCKS + 1:]

    @pl.when(pl.program_id(0) == 0)
    def _():
        for k, w_ref in enumerate(w_refs):
            w_s[:, k * INPROJ_BLOCK:k * INPROJ_BLOCK + w_ref.shape[1]] = w_ref[...].astype(BF16)

    h = _rms(x_ref[...], g_ref[...]).astype(BF16)
    c_qk, c_v, c_gc = S5_WIDTH, S5_WIDTH + 2 * MLSTM_WIDTH, INPROJ_BLOCKS * INPROJ_BLOCK
    u_ref[...] = _dot(h, w_s[:, :c_qk])
    qk_ref[...] = _dot(h, w_s[:, c_qk:c_v])
    v_ref[...] = _dot(h, w_s[:, c_v:c_gc])
    gc_ref[...] = _dot(h, w_s[:, c_gc:])


def _inproj(x, g, w_in, layer):
    rows = x.shape[0]
    tm = min(ROW_TILE, rows)
    rb = lambda i: (i, 0)
    widths = (S5_WIDTH, 2 * MLSTM_WIDTH, MLSTM_WIDTH, LANES)
    col_block = lambda k, width: pl.BlockSpec((None, D_MODEL, width), lambda i: (layer, 0, k),
                                              pipeline_mode=pl.Buffered(1))
    w_specs = [col_block(k, INPROJ_BLOCK) for k in range(INPROJ_BLOCKS)] + [col_block(GATE_COL // LANES, LANES)]
    return pl.pallas_call(
        _inproj_body,
        grid=(rows // tm,),
        in_specs=[pl.BlockSpec((tm, D_MODEL), rb), _layer_spec((1, D_MODEL), layer)] + w_specs,
        out_specs=[pl.BlockSpec((tm, w), rb) for w in widths],
        out_shape=[jax.ShapeDtypeStruct((rows, w), F32) for w in widths],
        scratch_shapes=[pltpu.VMEM((D_MODEL, INPROJ_BLOCKS * INPROJ_BLOCK + LANES), BF16)],
        compiler_params=_params(("arbitrary",)),
        name="inproj",
    )(x, g, *([w_in] * (INPROJ_BLOCKS + 1)))


def _s5_param_body(lre_ref, lim_ref, ldt_ref, bre_ref, bim_ref, are_ref, aim_ref, bbre_ref, bbim_ref):
    lre = lre_ref[0]
    lim = lim_ref[0]
    dt = jnp.exp(ldt_ref[0])
    mag = jnp.exp(lre * dt)
    a_re = mag * jnp.cos(lim * dt)
    a_im = mag * jnp.sin(lim * dt)
    den = lre * lre + lim * lim
    pr = a_re - 1.0
    w_re = (pr * lre + a_im * lim) / den
    w_im = (a_im * lre - pr * lim) / den
    are_ref[0] = a_re
    aim_ref[0] = a_im
    bbre_ref[0] = w_re * bre_ref[0] - w_im * bim_ref[0]
    bbim_ref[0] = w_re * bim_ref[0] + w_im * bre_ref[0]


def _s5_params(lre, lim, ldt, bre, bim):
    depth = lre.shape[0]
    vec = pl.BlockSpec((1, 1, S5_LANES), lambda i: (i, 0, 0))
    mat = pl.BlockSpec((1, S5_GROUP, S5_LANES), lambda i: (i, 0, 0))
    return pl.pallas_call(
        _s5_param_body,
        grid=(depth,),
        in_specs=[vec, vec, vec, mat, mat],
        out_specs=[vec, vec, mat, mat],
        out_shape=[jax.ShapeDtypeStruct((depth, 1, S5_LANES), F32)] * 2
        + [jax.ShapeDtypeStruct((depth, S5_GROUP, S5_LANES), F32)] * 2,
        compiler_params=_params(("parallel",)),
        name="s5_params",
    )(lre, lim, ldt, bre, bim)


def _s5_body(*refs, n_batch, n_time, has_state, n_prev):
    weights = refs[:9]
    refs = refs[9:]
    h0 = None
    if has_state:
        h0 = refs[:2]
        refs = refs[2:]
    outs = refs[n_prev:n_prev + 3]
    _s5_init(outs[1], outs[2], h0)
    _s5_chunk(*weights, *outs, *refs[n_prev + 3:], n_batch=n_batch, n_time=n_time)


def _s5_init(sr_ref, si_ref, h0):
    @pl.when(pl.program_id(0) == 0)
    def _():
        if h0 is not None:
            sr_ref[...] = h0[0][...]
            si_ref[...] = h0[1][...]
        else:
            sr_ref[...] = jnp.zeros_like(sr_ref)
            si_ref[...] = jnp.zeros_like(si_ref)


def _s5_chunk(u_ref, perm_ref, permt_ref, are_ref, aim_ref, bm_ref, cre_ref, cim_ref, d_ref,
              y_ref, sr_ref, si_ref, ub_s, bur_s, bui_s, hr_s, hi_s, y_s, *, n_batch, n_time):
    rows = n_batch * n_time
    n_bt = n_batch // SUBLANES

    u = u_ref[...].reshape(rows, S5_WIDTH)
    ub = _dot(perm_ref[...], u.astype(BF16)).astype(BF16)
    for j in range(S5_BLOCKS):
        ub_s[j] = ub[:, j * S5_BLOCK_IN:(j + 1) * S5_BLOCK_IN]

    def project_in(j):
        r = _dot(ub_s[j], bm_ref[j])
        bur_s[j] = r[:, :S5_BLOCK_ST]
        bui_s[j] = r[:, S5_BLOCK_ST:]

    def scan(j):
        a_re = jnp.broadcast_to(are_ref[j], (SUBLANES, S5_BLOCK_ST))
        a_im = jnp.broadcast_to(aim_ref[j], (SUBLANES, S5_BLOCK_ST))

        def advance(h, row):
            h_re, h_im = h
            tile = slice(row, row + SUBLANES)
            return (a_re * h_re - a_im * h_im + bur_s[j, tile, :],
                    a_re * h_im + a_im * h_re + bui_s[j, tile, :])

        def emit(row, first, second):
            pair = slice(row, row + 2 * SUBLANES)
            hr_s[j, pair, :] = jnp.concatenate([first[0], second[0]], axis=0).astype(BF16)
            hi_s[j, pair, :] = jnp.concatenate([first[1], second[1]], axis=0).astype(BF16)

        if n_bt == 1:
            h = (sr_ref[j], si_ref[j])
            for t in range(0, n_time, 2):
                h1 = advance(h, t * SUBLANES)
                h = advance(h1, (t + 1) * SUBLANES)
                emit(t * SUBLANES, h1, h)
            sr_ref[j], si_ref[j] = h
        else:
            for bt in range(0, n_bt, 2):
                rows_a = slice(bt * SUBLANES, (bt + 1) * SUBLANES)
                rows_b = slice((bt + 1) * SUBLANES, (bt + 2) * SUBLANES)
                ha = (sr_ref[j, rows_a, :], si_ref[j, rows_a, :])
                hb = (sr_ref[j, rows_b, :], si_ref[j, rows_b, :])
                for t in range(n_time):
                    row = t * n_batch + bt * SUBLANES
                    ha = advance(ha, row)
                    hb = advance(hb, row + SUBLANES)
                    emit(row, ha, hb)
                sr_ref[j, rows_a, :], si_ref[j, rows_a, :] = ha
                sr_ref[j, rows_b, :], si_ref[j, rows_b, :] = hb

    def project_out(j):
        y_s[j] = _dot(hr_s[j], cre_ref[j]) - _dot(hi_s[j], cim_ref[j])

    project_in(0)

    def pipelined(j, carry):
        project_in(j + 1)
        scan(j)
        project_out(j)
        return carry

    lax.fori_loop(0, S5_BLOCKS - 1, pipelined, 0)
    scan(S5_BLOCKS - 1)
    project_out(S5_BLOCKS - 1)
    y = jnp.concatenate([y_s[j] for j in range(S5_BLOCKS)], axis=1)
    y_hi = y.astype(BF16)
    y_lo = (y - y_hi.astype(F32)).astype(BF16)
    permt = permt_ref[...]
    y = _dot(permt, y_hi) + _dot(permt, y_lo) + d_ref[...] * u
    y_ref[...] = y.reshape(y_ref.shape)


def _s5(u, perm, permt, a_re, a_im, bm, cre, cim, d, h0, prev, layer, depth, n_batch, n_time):
    seq = u.shape[0] // n_batch
    rows = n_batch * n_time
    has_state = h0 is not None
    assert n_batch % (2 * SUBLANES) == 0 or (n_batch == SUBLANES and n_time % 2 == 0)
    lead, blk_rows = (n_batch, n_time) if seq > n_time else (1, rows)
    u = u.reshape(lead, u.shape[0] // lead, S5_WIDTH)
    u_spec = pl.BlockSpec((lead, blk_rows, S5_WIDTH), lambda c: (0, c, 0))
    state_spec = pl.BlockSpec((None, S5_BLOCKS, n_batch, S5_BLOCK_ST), lambda c: (layer, 0, 0, 0))
    state_shape = jax.ShapeDtypeStruct((depth, S5_BLOCKS, n_batch, S5_BLOCK_ST), F32)
    in_specs = [u_spec,
                _const_spec((rows, rows)), _const_spec((rows, rows)),
                _layer_spec((S5_BLOCKS, 1, S5_BLOCK_ST), layer), _layer_spec((S5_BLOCKS, 1, S5_BLOCK_ST), layer),
                _layer_spec((S5_BLOCKS, S5_BLOCK_IN, 2 * S5_BLOCK_ST), layer),
                _layer_spec((S5_BLOCKS, S5_BLOCK_ST, S5_BLOCK_IN), layer),
                _layer_spec((S5_BLOCKS, S5_BLOCK_ST, S5_BLOCK_IN), layer),
                _layer_spec((1, S5_WIDTH), layer)]
    args = [u, perm, permt, a_re, a_im, bm, cre, cim, d]
    if has_state:
        in_specs += [state_spec, state_spec]
        args += list(h0)
    aliases = {len(args) + k: 1 + k for k in range(len(prev))}
    in_specs += [_ANY] * len(prev)
    args += list(prev)
    return pl.pallas_call(
        functools.partial(_s5_body, n_batch=n_batch, n_time=n_time, has_state=has_state, n_prev=len(prev)),
        grid=(seq // n_time,),
        in_specs=in_specs,
        out_specs=[u_spec, state_spec, state_spec],
        out_shape=[jax.ShapeDtypeStruct(u.shape, F32), state_shape, state_shape],
        scratch_shapes=[pltpu.VMEM((S5_BLOCKS, rows, S5_BLOCK_IN), BF16),
                        pltpu.VMEM((S5_BLOCKS, rows, S5_BLOCK_ST), F32),
                        pltpu.VMEM((S5_BLOCKS, rows, S5_BLOCK_ST), F32),
                        pltpu.VMEM((S5_BLOCKS, rows, S5_BLOCK_ST), BF16),
                        pltpu.VMEM((S5_BLOCKS, rows, S5_BLOCK_ST), BF16),
                        pltpu.VMEM((S5_BLOCKS, rows, S5_BLOCK_IN), F32)],
        input_output_aliases=aliases,
        compiler_params=_params(("arbitrary",)),
        name="s5",
    )(*args)


def _time_major_perm(n_batch, n_time):
    r = jnp.arange(n_batch * n_time)
    src = (r % n_batch) * n_time + r // n_batch
    return (src[:, None] == r[None, :]).astype(BF16)


def _mlstm_tile(g, qk_ref, v_ref, gc_ref, cw_ref, cb_ref, bc_ref, gm_ref,
                c_in, n_in, m_in, hm_ref, c_ref, n_ref, m_ref, conv_ref, xp_s, qk_s,
                *, n_sub, seq_len, carry_hist):
    chunk = n_sub * seq_len
    region = SUBLANES + -(-seq_len // SUBLANES) * SUBLANES
    hist = SUBLANES
    slots = [g * n_sub + j for j in range(n_sub)]

    for j, slot in enumerate(slots):
        base = slot * region
        x_raw = qk_ref[g, j * seq_len:(j + 1) * seq_len, :]
        xp_s[base + hist:base + hist + seq_len, :] = x_raw
        if seq_len % SUBLANES == 0:
            x3 = x_raw.reshape(seq_len // SUBLANES, SUBLANES, 2 * MLSTM_WIDTH)
            prev = xp_s[base:base + hist, :]
            sub = lax.broadcasted_iota(jnp.int32, (1, SUBLANES, 1), 1)
            acc = cb_ref[...] + x3 * cw_ref[CONV_HIST:CONV_WIDTH, :]
            for k in range(1, CONV_WIDTH):
                rot = pltpu.roll(x3, k, axis=1)
                before = jnp.concatenate([pltpu.roll(prev, k, axis=0)[None], rot[:-1]], axis=0)
                acc = acc + jnp.where(sub < k, before, rot) * cw_ref[CONV_HIST - k:CONV_WIDTH - k, :]
            acc = acc.reshape(seq_len, 2 * MLSTM_WIDTH)
        else:
            acc = jnp.broadcast_to(cb_ref[...], (seq_len, 2 * MLSTM_WIDTH))
            for i in range(CONV_WIDTH):
                off = base + hist - CONV_HIST + i
                acc = acc + xp_s[off:off + seq_len, :] * cw_ref[i:i + 1, :]
        new_hist = xp_s[base + hist + seq_len - CONV_HIST:base + hist + seq_len, :]
        conv_ref[slot] = new_hist
        if carry_hist:
            xp_s[base + hist - CONV_HIST:base + hist, :] = new_hist
        if n_sub > 1:
            qk_s[g, j * seq_len:(j + 1) * seq_len, :] = acc
        yield
    qk = jax.nn.silu(qk_s[g] if n_sub > 1 else acc)
    q_all = qk[:, :MLSTM_WIDTH] * (MLSTM_HEAD_DIM ** -0.5)
    k_all = qk[:, MLSTM_WIDTH:]
    v_all = v_ref[g]
    yield

    def seq_of(idx):
        s = jnp.zeros(idx.shape, F32)
        for j in range(1, n_sub):
            s = s + jnp.where(idx >= j * seq_len, 1.0, 0.0)
        return s

    row_seq = seq_of(lax.broadcasted_iota(jnp.int32, (chunk, 1), 0))
    col_seq = seq_of(lax.broadcasted_iota(jnp.int32, (1, chunk), 1))
    t_id = lax.broadcasted_iota(jnp.int32, (chunk, chunk), 0)
    s_id = lax.broadcasted_iota(jnp.int32, (chunk, chunk), 1)
    same = jnp.where(row_seq == col_seq, 1.0, 0.0) if n_sub > 1 else jnp.ones((chunk, chunk), F32)
    tril = jnp.where(s_id <= t_id, same, 0.0)
    triu = jnp.where(t_id <= s_id, same, 0.0)
    causal = tril > 0.5
    pick = (lax.broadcasted_iota(jnp.int32, (SUBLANES, LANES), 0)
            == lax.broadcasted_iota(jnp.int32, (SUBLANES, LANES), 1)).astype(F32)

    li_col = gc_ref[g] + bc_ref[...]
    li_row = _dot_nt(pick, li_col, precision=HIGHEST)
    lf_col = jax.nn.log_sigmoid(li_col)
    yield
    lf_row = jax.nn.log_sigmoid(li_row)
    b_col = jnp.dot(tril, lf_col, preferred_element_type=F32, precision=HIGHEST)
    b_row = jnp.dot(lf_row, triu, preferred_element_type=F32, precision=HIGHEST)
    yield

    def per_row(vals):
        out = vals[0]
        for j in range(1, n_sub):
            out = jnp.where(row_seq == j, vals[j], out)
        return out

    m_new_parts = [[] for _ in range(n_sub)]
    for h in range(MLSTM_HEADS):
        hs = slice(h * MLSTM_HEAD_DIM, (h + 1) * MLSTM_HEAD_DIM)
        bc = b_col[:, MLSTM_HEADS + h:MLSTM_HEADS + h + 1]
        br = b_row[MLSTM_HEADS + h:MLSTM_HEADS + h + 1, :]
        lic = li_col[:, h:h + 1]
        lir = li_row[h:h + 1, :]
        m_prev = [m_in[s][:, h:h + 1] for s in slots]
        c_old = [c_in[s, h] for s in slots]
        n_old = [n_in[s, h:h + 1, :] for s in slots]
        qf = q_all[:, hs]
        kf = k_all[:, hs]
        qb = qf.astype(BF16)
        vb = v_all[:, hs].astype(BF16)

        dmat = jnp.where(causal, bc - br + lir, -jnp.inf)
        inter = bc + per_row(m_prev)
        m_t = jnp.maximum(inter, jnp.max(dmat, axis=-1, keepdims=True))
        s_qk = _dot_nt(qb, kf.astype(BF16))
        q_c = per_row([_dot(qb, c.astype(BF16)) for c in c_old])
        yield
        w = jnp.exp(dmat - m_t)
        scores = s_qk * w
        a = jnp.exp(inter - m_t)
        num = _dot(scores.astype(BF16), vb) + a * q_c
        nq = (jnp.sum(scores, axis=-1, keepdims=True)
              + a * jnp.sum(qf * per_row(n_old), axis=-1, keepdims=True))
        yield
        hout = num / jnp.maximum(jnp.abs(nq), jnp.exp(-m_t))
        hout = hout * lax.rsqrt(jnp.mean(hout * hout, axis=-1, keepdims=True) + EPS)
        hm_ref[g, :, hs] = hout * gm_ref[:, hs]
        yield

        for j, slot in enumerate(slots):
            last = (j + 1) * seq_len - 1
            b_last = br[:, last:last + 1]
            g_row = b_last - br + lir
            if n_sub > 1:
                g_row = jnp.where(col_seq == j, g_row, -jnp.inf)
            m_new = jnp.maximum(b_last + m_prev[j], jnp.max(g_row, axis=-1, keepdims=True))
            wk = jnp.exp(b_last - bc + lic - m_new)
            if n_sub > 1:
                wk = jnp.where(row_seq == j, wk, 0.0)
            decay = jnp.exp(b_last + m_prev[j] - m_new)
            kw = kf * wk
            c_ref[slot, h] = decay * c_old[j] + _dot_tn(kw.astype(BF16), vb)
            n_ref[slot, h:h + 1, :] = decay * n_old[j] + jnp.sum(kw, axis=0, keepdims=True)
            m_new_parts[j].append(m_new)
            yield
    for j, slot in enumerate(slots):
        m_ref[slot] = jnp.concatenate(m_new_parts[j], axis=1)


def _mlstm_body(*refs, n_sub, seq_len, has_state, n_prev, single_chunk, par):
    qk_ref, v_ref, gc_ref, cw_ref, cb_ref, bc_ref, gm_ref = refs[:7]
    refs = refs[7:]
    c0_ref = n0_ref = m0_ref = conv0_ref = None
    if has_state:
        c0_ref, n0_ref, m0_ref, conv0_ref = refs[:4]
        refs = refs[4:]
    hm_ref, c_ref, n_ref, m_ref, conv_ref = refs[n_prev:n_prev + 5]
    scratch = refs[n_prev + 5:]
    xp_s = scratch[0]
    qk_s = scratch[1] if n_sub > 1 else None

    from_input = has_state and single_chunk
    c_in, n_in, m_in = (c0_ref, n0_ref, m0_ref) if from_input else (c_ref, n_ref, m_ref)

    @pl.when(pl.program_id(1) == 0)
    def _():
        _mlstm_init(c_ref, n_ref, m_ref, xp_s, None if from_input else (c0_ref, n0_ref, m0_ref), conv0_ref,
                    state_from_input=from_input)

    tiles = [_mlstm_tile(g, qk_ref, v_ref, gc_ref, cw_ref, cb_ref, bc_ref, gm_ref,
                         c_in, n_in, m_in, hm_ref, c_ref, n_ref, m_ref, conv_ref, xp_s, qk_s,
                         n_sub=n_sub, seq_len=seq_len, carry_hist=not single_chunk) for g in range(par)]
    order = itertools.zip_longest(*tiles) if single_chunk else itertools.chain(*tiles)
    for _ in order:
        pass


def _mlstm_init(c_ref, n_ref, m_ref, xp_s, state0, conv0_ref, *, state_from_input):
    n_slots = c_ref.shape[0]
    region = xp_s.shape[0] // n_slots
    for slot in range(n_slots):
        xp_s[slot * region:slot * region + SUBLANES, :] = jnp.zeros((SUBLANES, 2 * MLSTM_WIDTH), F32)
        if conv0_ref is not None:
            xp_s[slot * region + SUBLANES - CONV_HIST:slot * region + SUBLANES, :] = conv0_ref[slot]
    if state_from_input:
        return
    for ref, ref0 in zip((c_ref, n_ref, m_ref), state0):
        ref[...] = jnp.zeros_like(ref) if ref0 is None else ref0[...]


def _mlstm(qk, v, gc, cw, cb, bias_c, gm, state, prev, layer, depth, n_tiles, n_sub, seq_len, n_chunk, par):
    assert n_sub == 1 or n_chunk == 1
    assert n_tiles % par == 0
    hh, dh = MLSTM_HEADS, MLSTM_HEAD_DIM
    chunk = n_sub * seq_len
    n_slots = par * n_sub
    n_all = n_tiles * n_sub
    has_state = state is not None
    tile3 = lambda t: t.reshape(n_tiles, n_chunk * chunk, t.shape[-1])
    act_spec = lambda w: pl.BlockSpec((par, chunk, w), lambda b, c: (b, c, 0))
    st5 = lambda b, c: (layer, b, 0, 0, 0)
    st4 = lambda b, c: (layer, b, 0, 0)
    state_specs = [pl.BlockSpec((None, n_slots, hh, dh, dh), st5),
                   pl.BlockSpec((None, n_slots, hh, dh), st4),
                   pl.BlockSpec((None, n_slots, 1, hh), st4),
                   pl.BlockSpec((None, n_slots, CONV_HIST, 2 * MLSTM_WIDTH), st4)]
    in_specs = [act_spec(2 * MLSTM_WIDTH), act_spec(MLSTM_WIDTH), act_spec(LANES),
                _layer_spec((CONV_WIDTH, 2 * MLSTM_WIDTH), layer), _layer_spec((1, 2 * MLSTM_WIDTH), layer),
                _layer_spec((1, LANES), layer), _layer_spec((1, MLSTM_WIDTH), layer)]
    args = [tile3(qk), tile3(v), tile3(gc), cw, cb, bias_c, gm]
    if has_state:
        in_specs += state_specs
        args += list(state)
    aliases = {len(args) + k: 1 + k for k in range(len(prev))}
    in_specs += [_ANY] * len(prev)
    args += list(prev)
    out_shape = [jax.ShapeDtypeStruct((n_tiles, n_chunk * chunk, MLSTM_WIDTH), F32),
                 jax.ShapeDtypeStruct((depth, n_all, hh, dh, dh), F32),
                 jax.ShapeDtypeStruct((depth, n_all, hh, dh), F32),
                 jax.ShapeDtypeStruct((depth, n_all, 1, hh), F32),
                 jax.ShapeDtypeStruct((depth, n_all, CONV_HIST, 2 * MLSTM_WIDTH), F32)]
    region = SUBLANES + -(-seq_len // SUBLANES) * SUBLANES
    scratch = [pltpu.VMEM((n_slots * region, 2 * MLSTM_WIDTH), F32)]
    if n_sub > 1:
        scratch.append(pltpu.VMEM((par, chunk, 2 * MLSTM_WIDTH), F32))
    return pl.pallas_call(
        functools.partial(_mlstm_body, n_sub=n_sub, seq_len=seq_len, has_state=has_state,
                          n_prev=len(prev), single_chunk=n_chunk == 1, par=par),
        grid=(n_tiles // par, n_chunk),
        in_specs=in_specs,
        out_specs=[act_spec(MLSTM_WIDTH)] + state_specs,
        out_shape=out_shape,
        scratch_shapes=scratch,
        input_output_aliases=aliases,
        compiler_params=_params(("parallel", "arbitrary")),
        name="mlstm",
    )(*args)


def _post_body(x_ref, y_ref, hm_ref, g_ref, wo_ref, wgs_ref, wgm_ref, wglu_ref, wsu_ref, wmu_ref,
               wout_ref, o_ref):
    x = x_ref[...]
    h = _rms(x, g_ref[...]).astype(BF16)
    ys = jax.nn.gelu(y_ref[...])
    ys = ys * jax.nn.sigmoid(_dot(ys.astype(BF16), wglu_ref[...]))
    hm = hm_ref[...] * jax.nn.sigmoid(_dot(h, wo_ref[...]))
    merged = (jax.nn.sigmoid(_dot(h, wgs_ref[...])) * _dot(ys.astype(BF16), wsu_ref[...])
              + jax.nn.sigmoid(_dot(h, wgm_ref[...])) * _dot(hm.astype(BF16), wmu_ref[...]))
    o_ref[...] = x + _dot(merged.astype(BF16), wout_ref[...])


def _post(x, y, hm, g, wo, wgs, wgm, wglu, wsu, wmu, wout, layer):
    rows = x.shape[0]
    tm = min(ROW_TILE, rows)
    rb = lambda i: (i, 0)
    sq = _layer_spec((D_MODEL, D_MODEL), layer)
    return pl.pallas_call(
        _post_body,
        grid=(rows // tm,),
        in_specs=[pl.BlockSpec((tm, D_MODEL), rb), pl.BlockSpec((tm, S5_WIDTH), rb),
                  pl.BlockSpec((tm, MLSTM_WIDTH), rb),
                  _layer_spec((1, D_MODEL), layer), sq, sq, sq,
                  _layer_spec((S5_WIDTH, S5_WIDTH), layer), _layer_spec((S5_WIDTH, D_MODEL), layer), sq, sq],
        out_specs=pl.BlockSpec((tm, D_MODEL), rb),
        out_shape=jax.ShapeDtypeStruct((rows, D_MODEL), F32),
        compiler_params=_params(("parallel",)),
        name="post",
    )(x, y, hm, g, wo, wgs, wgm, wglu, wsu, wmu, wout)


def _block_diag(t):
    depth, nb, gpb, a, c = t.shape
    eye = jnp.eye(gpb, dtype=t.dtype)
    return jnp.einsum("ljgac,gk->ljgakc", t, eye).reshape(depth, nb, gpb * a, gpb * c)


def kernel(x_prompt, x_sample, state_s5_re, state_s5_im, state_mlstm_c, state_mlstm_n, state_mlstm_m, state_conv, p_prompt, p_sample, g_ffn1, w1_gate, w1_up, w1_down, g_mix, w_in, s5_lambda_re, s5_lambda_im, s5_log_dt, s5_b_re, s5_b_im, s5_c_re, s5_c_im, s5_d, s5_w_glu, w_s5_up, conv_w, conv_b, b_igate, b_fgate, g_mhead, w_m_up, w_out, g_ffn2, w2_gate, w2_up, w2_down, g_ple, w_ple, w_ple_gate, g_final):
    depth = w_in.shape[0]
    bp, sp, _ = x_prompt.shape
    bs, ss, _ = x_sample.shape
    hh = MLSTM_HEADS
    gpb = S5_GROUPS // S5_BLOCKS

    bf = lambda w: w.astype(BF16)
    c_gate = GATE_COL + 2 * MLSTM_HEADS
    w_o = bf(w_in[:, :, GATE_COL - MLSTM_WIDTH:GATE_COL])
    w_gs = bf(w_in[:, :, c_gate:c_gate + D_MODEL])
    w_gm = bf(w_in[:, :, c_gate + D_MODEL:])
    w2g, w2u, w2d = bf(w2_gate), bf(w2_up), bf(w2_down)
    wglu, wsu, wmu, wout = bf(s5_w_glu), bf(w_s5_up), bf(w_m_up), bf(w_out)
    wple, wpg = bf(w_ple), bf(w_ple_gate)
    row = lambda g: g.reshape(depth, 1, -1)
    gf1, gmx, gf2, gpl, gmh, s5d, cvb = (row(g_ffn1), row(g_mix), row(g_ffn2), row(g_ple),
                                          row(g_mhead), row(s5_d), row(conv_b))
    gfin = g_final.reshape(1, D_MODEL)
    bias_c = jnp.pad(jnp.concatenate([b_igate, b_fgate], axis=1),
                     ((0, 0), (0, LANES - 2 * MLSTM_HEADS))).reshape(depth, 1, LANES)

    lanes3 = lambda t: t.reshape(depth, 1, S5_LANES)
    ldt = jnp.broadcast_to(s5_log_dt[:, :, None], (depth, S5_GROUPS, S5_STATE))
    to_cols = lambda t: jnp.transpose(t, (0, 3, 1, 2)).reshape(depth, S5_GROUP, S5_LANES)
    a_re, a_im, bb_re, bb_im = _s5_params(lanes3(s5_lambda_re), lanes3(s5_lambda_im), lanes3(ldt),
                                          to_cols(s5_b_re), to_cols(s5_b_im))
    to_blk = lambda t: jnp.transpose(t.reshape(depth, S5_GROUP, S5_BLOCKS, gpb, S5_STATE), (0, 2, 3, 1, 4))
    bmat = bf(jnp.concatenate([_block_diag(to_blk(bb_re)), _block_diag(to_blk(bb_im))], axis=-1))
    c_blk = lambda t: jnp.transpose(t.reshape(depth, S5_BLOCKS, gpb, S5_GROUP, S5_STATE), (0, 1, 2, 4, 3))
    cmat_re = bf(_block_diag(c_blk(s5_c_re)))
    cmat_im = bf(_block_diag(c_blk(s5_c_im)))
    s5_time_p = S5_ROWS // bp
    perm_p = _time_major_perm(bp, s5_time_p)
    perm_s = _time_major_perm(bs, ss)

    xp = x_prompt.reshape(bp * sp, D_MODEL)
    xs = x_sample.reshape(bs * ss, D_MODEL)
    pp = p_prompt.reshape(depth, bp * sp, D_PLE)
    ps = p_sample.reshape(depth, bs * ss, D_PLE)
    to_blocks = lambda t: jnp.swapaxes(t.reshape(depth, -1, S5_BLOCKS, S5_BLOCK_ST), 1, 2)
    from_blocks = lambda t: jnp.swapaxes(t, 1, 2).reshape(depth, -1, S5_GROUPS, S5_STATE)
    a_re = a_re.reshape(depth, S5_BLOCKS, 1, S5_BLOCK_ST)
    a_im = a_im.reshape(depth, S5_BLOCKS, 1, S5_BLOCK_ST)
    s5_state = (to_blocks(state_s5_re), to_blocks(state_s5_im))
    lstm_state = (state_mlstm_c, state_mlstm_n, state_mlstm_m.reshape(depth, bs, 1, hh), state_conv)

    s5_out_p, s5_out_s, lstm_out_p, lstm_out_s = [], [], [], []
    for i in range(depth):
        final = i == depth - 1
        ffn1_w = (gf1, w1_gate, w1_up, w1_down)
        proj_w = (gmx, w_in)
        s5_w = (a_re, a_im, bmat, cmat_re, cmat_im, s5d)
        lstm_w = (conv_w, cvb, bias_c, gmh)
        mix_w = (gmx, w_o, w_gs, w_gm, wglu, wsu, wmu, wout)
        ple_w = (gf2, w2g, w2u, w2d, gpl, wple, wpg, gfin)

        xp = _ffn(xp, *ffn1_w, layer=i)
        u, qk, v, gc = _inproj(xp, *proj_w, layer=i)
        y, *s5_out_p = _s5(u, perm_p, perm_p.T, *s5_w, None, s5_out_p,
                           layer=i, depth=depth, n_batch=bp, n_time=s5_time_p)
        hm, *lstm_out_p = _mlstm(qk, v, gc, *lstm_w, None, lstm_out_p, layer=i, depth=depth,
                                 n_tiles=bp, n_sub=1, seq_len=MLSTM_CHUNK, n_chunk=sp // MLSTM_CHUNK,
                                 par=PROMPT_TILES_PER_STEP)
        xp = _post(xp, y.reshape(bp * sp, S5_WIDTH), hm.reshape(bp * sp, MLSTM_WIDTH), *mix_w, layer=i)
        xp = _ffn_ple(xp, pp, *ple_w, layer=i, final=final)

        xs = _ffn(xs, *ffn1_w, layer=i)
        u, qk, v, gc = _inproj(xs, *proj_w, layer=i)
        y, *s5_out_s = _s5(u, perm_s, perm_s.T, *s5_w, s5_state, s5_out_s,
                           layer=i, depth=depth, n_batch=bs, n_time=ss)
        hm, *lstm_out_s = _mlstm(qk, v, gc, *lstm_w, lstm_state, lstm_out_s, layer=i, depth=depth,
                                 n_tiles=bs // SAMPLE_SEQS_PER_TILE, n_sub=SAMPLE_SEQS_PER_TILE,
                                 seq_len=ss, n_chunk=1, par=SAMPLE_TILES_PER_STEP)
        xs = _post(xs, y.reshape(bs * ss, S5_WIDTH), hm.reshape(bs * ss, MLSTM_WIDTH), *mix_w, layer=i)
        xs = _ffn_ple(xs, ps, *ple_w, layer=i, final=final)

    def states(n, s5_out, lstm_out):
        c_n, n_n, m_n, conv_n = lstm_out
        return (from_blocks(s5_out[0]), from_blocks(s5_out[1]), c_n, n_n, m_n.reshape(depth, n, hh), conv_n)

    return ((xp.reshape(bp, sp, D_MODEL), xs.reshape(bs, ss, D_MODEL))
            + states(bp, s5_out_p, lstm_out_p) + states(bs, s5_out_s, lstm_out_s))
```

```python
import functools
import itertools

import jax
import jax.numpy as jnp
from jax import lax
from jax.experimental import pallas as pl
from jax.experimental.pallas import tpu as pltpu

D_MODEL = 1024
D_PLE = 256
D_FF = 2048
S5_WIDTH = 512
S5_GROUP = 16
S5_GROUPS = S5_WIDTH // S5_GROUP
S5_STATE = 64
S5_LANES = S5_GROUPS * S5_STATE
MLSTM_WIDTH = 1024
MLSTM_HEADS = 4
MLSTM_HEAD_DIM = MLSTM_WIDTH // MLSTM_HEADS
MLSTM_CHUNK = 128
CONV_WIDTH = 4
CONV_HIST = CONV_WIDTH - 1
EPS = 1e-6

F32 = jnp.float32
BF16 = jnp.bfloat16
HIGHEST = lax.Precision.HIGHEST

SUBLANES = 8
LANES = 128
VMEM_LIMIT_BYTES = 56 * 1024 * 1024

ROW_TILE = 512
S5_ROWS = 512
S5_BLOCKS = 4
S5_BLOCK_IN = S5_WIDTH // S5_BLOCKS
S5_BLOCK_ST = S5_LANES // S5_BLOCKS
SAMPLE_SEQS_PER_TILE = 2
PROMPT_TILES_PER_STEP = 4
SAMPLE_TILES_PER_STEP = 4


def _params(sem):
    return pltpu.CompilerParams(dimension_semantics=sem, vmem_limit_bytes=VMEM_LIMIT_BYTES)


def _layer_spec(shape, layer):
    nd = len(shape)
    return pl.BlockSpec((None,) + tuple(shape), lambda *_: (layer,) + (0,) * nd,
                        pipeline_mode=pl.Buffered(1))


def _const_spec(shape):
    nd = len(shape)
    return pl.BlockSpec(tuple(shape), lambda *_: (0,) * nd, pipeline_mode=pl.Buffered(1))


_ANY = pl.BlockSpec(memory_space=pl.ANY)


def _rms(x, g):
    return x * lax.rsqrt(jnp.mean(x * x, axis=-1, keepdims=True) + EPS) * g


def _dot(a, b):
    return jnp.dot(a, b, preferred_element_type=F32)


def _dot_nt(a, b, **kw):
    return lax.dot_general(a, b, (((1,), (1,)), ((), ())), preferred_element_type=F32, **kw)


def _dot_tn(a, b):
    return lax.dot_general(a, b, (((0,), (0,)), ((), ())), preferred_element_type=F32)


def _ffn_body(x_ref, g_ref, wg_ref, wu_ref, wd_ref, o_ref):
    x = x_ref[...]
    h = _rms(x, g_ref[...]).astype(BF16)
    z = (jax.nn.silu(_dot(h, wg_ref[...])) * _dot(h, wu_ref[...])).astype(BF16)
    o_ref[...] = x + 0.5 * _dot(z, wd_ref[...])


def _ffn_ple_body(x_ref, p_ref, g_ref, wg_ref, wu_ref, wd_ref, gp_ref, wp_ref, wpg_ref, gf_ref, o_ref,
                  *, final):
    x = x_ref[...]
    h = _rms(x, g_ref[...]).astype(BF16)
    z = (jax.nn.silu(_dot(h, wg_ref[...])) * _dot(h, wu_ref[...])).astype(BF16)
    x = x + 0.5 * _dot(z, wd_ref[...])
    gate = jax.nn.sigmoid(_dot(_rms(x, gp_ref[...]).astype(BF16), wpg_ref[...]))
    x = x + _dot(p_ref[...].astype(BF16), wp_ref[...]) * gate
    if final:
        x = _rms(x, gf_ref[...])
    o_ref[...] = x


def _ffn(x, g, wg, wu, wd, layer):
    rows = x.shape[0]
    tm = min(ROW_TILE, rows)
    row_spec = pl.BlockSpec((tm, D_MODEL), lambda i: (i, 0))
    return pl.pallas_call(
        _ffn_body,
        grid=(rows // tm,),
        in_specs=[row_spec, _layer_spec((1, D_MODEL), layer), _layer_spec((D_MODEL, D_FF), layer),
                  _layer_spec((D_MODEL, D_FF), layer), _layer_spec((D_FF, D_MODEL), layer)],
        out_specs=row_spec,
        out_shape=jax.ShapeDtypeStruct((rows, D_MODEL), F32),
        compiler_params=_params(("parallel",)),
        name="ffn",
    )(x, g, wg, wu, wd)


def _ffn_ple(x, p, g, wg, wu, wd, gp, wp, wpg, gf, layer, final):
    rows = x.shape[0]
    tm = min(ROW_TILE, rows)
    row_spec = pl.BlockSpec((tm, D_MODEL), lambda i: (i, 0))
    return pl.pallas_call(
        functools.partial(_ffn_ple_body, final=final),
        grid=(rows // tm,),
        in_specs=[row_spec, pl.BlockSpec((None, tm, D_PLE), lambda i: (layer, i, 0)),
                  _layer_spec((1, D_MODEL), layer), _layer_spec((D_MODEL, D_FF), layer),
                  _layer_spec((D_MODEL, D_FF), layer), _layer_spec((D_FF, D_MODEL), layer),
                  _layer_spec((1, D_MODEL), layer), _layer_spec((D_PLE, D_MODEL), layer),
                  _layer_spec((D_MODEL, D_MODEL), layer), _const_spec((1, D_MODEL))],
        out_specs=row_spec,
        out_shape=jax.ShapeDtypeStruct((rows, D_MODEL), F32),
        compiler_params=_params(("parallel",)),
        name="ffn_ple",
    )(x, p, g, wg, wu, wd, gp, wp, wpg, gf)


def _inproj_body(x_ref, g_ref, wu_ref, wqk_ref, wv_ref, wgc_ref, u_ref, qk_ref, v_ref, gc_ref):
    h = _rms(x_ref[...], g_ref[...]).astype(BF16)
    u_ref[...] = _dot(h, wu_ref[...])
    qk_ref[...] = _dot(h, wqk_ref[...])
    v_ref[...] = _dot(h, wv_ref[...]).astype(v_ref.dtype)
    gc_ref[...] = _dot(h, wgc_ref[...])


def _inproj(x, g, wu, wqk, wv, wgc, layer, v_dtype):
    rows = x.shape[0]
    tm = min(ROW_TILE, rows)
    rb = lambda i: (i, 0)
    widths = (S5_WIDTH, 2 * MLSTM_WIDTH, MLSTM_WIDTH, LANES)
    dtypes = (F32, F32, v_dtype, F32)
    return pl.pallas_call(
        _inproj_body,
        grid=(rows // tm,),
        in_specs=[pl.BlockSpec((tm, D_MODEL), rb), _layer_spec((1, D_MODEL), layer)]
        + [_layer_spec((D_MODEL, w), layer) for w in widths],
        out_specs=[pl.BlockSpec((tm, w), rb) for w in widths],
        out_shape=[jax.ShapeDtypeStruct((rows, w), dt) for w, dt in zip(widths, dtypes)],
        compiler_params=_params(("parallel",)),
        name="inproj",
    )(x, g, wu, wqk, wv, wgc)


def _s5_param_body(lre_ref, lim_ref, ldt_ref, bre_ref, bim_ref, are_ref, aim_ref, bbre_ref, bbim_ref):
    lre = lre_ref[0]
    lim = lim_ref[0]
    dt = jnp.exp(ldt_ref[0])
    mag = jnp.exp(lre * dt)
    a_re = mag * jnp.cos(lim * dt)
    a_im = mag * jnp.sin(lim * dt)
    den = lre * lre + lim * lim
    pr = a_re - 1.0
    w_re = (pr * lre + a_im * lim) / den
    w_im = (a_im * lre - pr * lim) / den
    are_ref[0] = a_re
    aim_ref[0] = a_im
    bbre_ref[0] = w_re * bre_ref[0] - w_im * bim_ref[0]
    bbim_ref[0] = w_re * bim_ref[0] + w_im * bre_ref[0]


def _s5_params(lre, lim, ldt, bre, bim):
    depth = lre.shape[0]
    vec = pl.BlockSpec((1, 1, S5_LANES), lambda i: (i, 0, 0))
    mat = pl.BlockSpec((1, S5_GROUP, S5_LANES), lambda i: (i, 0, 0))
    return pl.pallas_call(
        _s5_param_body,
        grid=(depth,),
        in_specs=[vec, vec, vec, mat, mat],
        out_specs=[vec, vec, mat, mat],
        out_shape=[jax.ShapeDtypeStruct((depth, 1, S5_LANES), F32)] * 2
        + [jax.ShapeDtypeStruct((depth, S5_GROUP, S5_LANES), F32)] * 2,
        compiler_params=_params(("parallel",)),
        name="s5_params",
    )(lre, lim, ldt, bre, bim)


def _s5_body(*refs, n_batch, n_time, has_state):
    weights = refs[:9]
    refs = refs[9:]
    h0 = None
    if has_state:
        h0 = refs[:2]
        refs = refs[2:]
    sr_ref, si_ref = refs[1:3]

    @pl.when(pl.program_id(0) == 0)
    def _():
        if h0 is not None:
            sr_ref[...] = h0[0][...]
            si_ref[...] = h0[1][...]
        else:
            sr_ref[...] = jnp.zeros_like(sr_ref)
            si_ref[...] = jnp.zeros_like(si_ref)

    _s5_chunk(*weights, *refs, n_batch=n_batch, n_time=n_time)


def _s5_chunk(u_ref, perm_ref, permt_ref, are_ref, aim_ref, bm_ref, cre_ref, cim_ref, d_ref,
              y_ref, sr_ref, si_ref, ub_s, bur_s, bui_s, hr_s, hi_s, y_s, *, n_batch, n_time):
    rows = n_batch * n_time
    n_bt = n_batch // SUBLANES

    u = u_ref[...].reshape(rows, S5_WIDTH)
    ub = _dot(perm_ref[...], u.astype(BF16)).astype(BF16)
    for j in range(S5_BLOCKS):
        ub_s[j] = ub[:, j * S5_BLOCK_IN:(j + 1) * S5_BLOCK_IN]

    def project_in(j):
        r = _dot(ub_s[j], bm_ref[j])
        bur_s[j] = r[:, :S5_BLOCK_ST]
        bui_s[j] = r[:, S5_BLOCK_ST:]

    def scan(j):
        a_re = jnp.broadcast_to(are_ref[j], (SUBLANES, S5_BLOCK_ST))
        a_im = jnp.broadcast_to(aim_ref[j], (SUBLANES, S5_BLOCK_ST))

        def advance(h, row):
            h_re, h_im = h
            tile = slice(row, row + SUBLANES)
            return (a_re * h_re - a_im * h_im + bur_s[j, tile, :],
                    a_re * h_im + a_im * h_re + bui_s[j, tile, :])

        def emit(row, first, second):
            pair = slice(row, row + 2 * SUBLANES)
            hr_s[j, pair, :] = jnp.concatenate([first[0], second[0]], axis=0).astype(BF16)
            hi_s[j, pair, :] = jnp.concatenate([first[1], second[1]], axis=0).astype(BF16)

        if n_bt == 1:
            h = (sr_ref[j], si_ref[j])
            for t in range(0, n_time, 2):
                h1 = advance(h, t * SUBLANES)
                h = advance(h1, (t + 1) * SUBLANES)
                emit(t * SUBLANES, h1, h)
            sr_ref[j], si_ref[j] = h
        else:
            for bt in range(0, n_bt, 2):
                rows_a = slice(bt * SUBLANES, (bt + 1) * SUBLANES)
                rows_b = slice((bt + 1) * SUBLANES, (bt + 2) * SUBLANES)
                ha = (sr_ref[j, rows_a, :], si_ref[j, rows_a, :])
                hb = (sr_ref[j, rows_b, :], si_ref[j, rows_b, :])
                for t in range(n_time):
                    row = t * n_batch + bt * SUBLANES
                    ha = advance(ha, row)
                    hb = advance(hb, row + SUBLANES)
                    emit(row, ha, hb)
                sr_ref[j, rows_a, :], si_ref[j, rows_a, :] = ha
                sr_ref[j, rows_b, :], si_ref[j, rows_b, :] = hb

    def project_out(j):
        y_s[j] = _dot(hr_s[j], cre_ref[j]) - _dot(hi_s[j], cim_ref[j])

    project_in(0)

    def pipelined(j, carry):
        project_in(j + 1)
        scan(j)
        project_out(j)
        return carry

    lax.fori_loop(0, S5_BLOCKS - 1, pipelined, 0)
    scan(S5_BLOCKS - 1)
    project_out(S5_BLOCKS - 1)
    y = jnp.concatenate([y_s[j] for j in range(S5_BLOCKS)], axis=1)
    y_hi = y.astype(BF16)
    y_lo = (y - y_hi.astype(F32)).astype(BF16)
    permt = permt_ref[...]
    y = _dot(permt, y_hi) + _dot(permt, y_lo) + d_ref[...] * u
    y_ref[...] = y.reshape(y_ref.shape)


def _s5(u, perm, permt, a_re, a_im, bm, cre, cim, d, h0, layer, n_batch, n_time):
    seq = u.shape[0] // n_batch
    rows = n_batch * n_time
    has_state = h0 is not None
    assert n_batch % (2 * SUBLANES) == 0 or (n_batch == SUBLANES and n_time % 2 == 0)
    lead, blk_rows = (n_batch, n_time) if seq > n_time else (1, rows)
    u = u.reshape(lead, u.shape[0] // lead, S5_WIDTH)
    u_spec = pl.BlockSpec((lead, blk_rows, S5_WIDTH), lambda c: (0, c, 0))
    state_in_spec = pl.BlockSpec((None, S5_BLOCKS, n_batch, S5_BLOCK_ST), lambda c: (layer, 0, 0, 0))
    state_spec = pl.BlockSpec((S5_BLOCKS, n_batch, S5_BLOCK_ST), lambda c: (0, 0, 0))
    state_shape = jax.ShapeDtypeStruct((S5_BLOCKS, n_batch, S5_BLOCK_ST), F32)
    in_specs = [u_spec,
                _const_spec((rows, rows)), _const_spec((rows, rows)),
                _layer_spec((S5_BLOCKS, 1, S5_BLOCK_ST), layer), _layer_spec((S5_BLOCKS, 1, S5_BLOCK_ST), layer),
                _layer_spec((S5_BLOCKS, S5_BLOCK_IN, 2 * S5_BLOCK_ST), layer),
                _layer_spec((S5_BLOCKS, S5_BLOCK_ST, S5_BLOCK_IN), layer),
                _layer_spec((S5_BLOCKS, S5_BLOCK_ST, S5_BLOCK_IN), layer),
                _layer_spec((1, S5_WIDTH), layer)]
    args = [u, perm, permt, a_re, a_im, bm, cre, cim, d]
    if has_state:
        in_specs += [state_in_spec, state_in_spec]
        args += list(h0)
    return pl.pallas_call(
        functools.partial(_s5_body, n_batch=n_batch, n_time=n_time, has_state=has_state),
        grid=(seq // n_time,),
        in_specs=in_specs,
        out_specs=[u_spec, state_spec, state_spec],
        out_shape=[jax.ShapeDtypeStruct(u.shape, F32), state_shape, state_shape],
        scratch_shapes=[pltpu.VMEM((S5_BLOCKS, rows, S5_BLOCK_IN), BF16),
                        pltpu.VMEM((S5_BLOCKS, rows, S5_BLOCK_ST), F32),
                        pltpu.VMEM((S5_BLOCKS, rows, S5_BLOCK_ST), F32),
                        pltpu.VMEM((S5_BLOCKS, rows, S5_BLOCK_ST), BF16),
                        pltpu.VMEM((S5_BLOCKS, rows, S5_BLOCK_ST), BF16),
                        pltpu.VMEM((S5_BLOCKS, rows, S5_BLOCK_IN), F32)],
        compiler_params=_params(("arbitrary",)),
        name="s5",
    )(*args)


def _time_major_perm(n_batch, n_time):
    r = jnp.arange(n_batch * n_time)
    src = (r % n_batch) * n_time + r // n_batch
    return (src[:, None] == r[None, :]).astype(BF16)


def _mlstm_tile(g, qk_ref, v_ref, gc_ref, cw_ref, cb_ref, bc_ref, gm_ref,
                c_in, n_in, m_in, hm_ref, c_ref, n_ref, m_ref, conv_ref, xp_s, qk_s,
                *, n_sub, seq_len, carry_hist):
    chunk = n_sub * seq_len
    region = SUBLANES + -(-seq_len // SUBLANES) * SUBLANES
    hist = SUBLANES
    slots = [g * n_sub + j for j in range(n_sub)]

    for j, slot in enumerate(slots):
        base = slot * region
        x_raw = qk_ref[g, j * seq_len:(j + 1) * seq_len, :]
        xp_s[base + hist:base + hist + seq_len, :] = x_raw
        if seq_len % SUBLANES == 0:
            x3 = x_raw.reshape(seq_len // SUBLANES, SUBLANES, 2 * MLSTM_WIDTH)
            prev = xp_s[base:base + hist, :]
            sub = lax.broadcasted_iota(jnp.int32, (1, SUBLANES, 1), 1)
            acc = cb_ref[...] + x3 * cw_ref[CONV_HIST:CONV_WIDTH, :]
            for k in range(1, CONV_WIDTH):
                rot = pltpu.roll(x3, k, axis=1)
                before = jnp.concatenate([pltpu.roll(prev, k, axis=0)[None], rot[:-1]], axis=0)
                acc = acc + jnp.where(sub < k, before, rot) * cw_ref[CONV_HIST - k:CONV_WIDTH - k, :]
            acc = acc.reshape(seq_len, 2 * MLSTM_WIDTH)
        else:
            acc = jnp.broadcast_to(cb_ref[...], (seq_len, 2 * MLSTM_WIDTH))
            for i in range(CONV_WIDTH):
                off = base + hist - CONV_HIST + i
                acc = acc + xp_s[off:off + seq_len, :] * cw_ref[i:i + 1, :]
        new_hist = xp_s[base + hist + seq_len - CONV_HIST:base + hist + seq_len, :]
        conv_ref[slot] = new_hist
        if carry_hist:
            xp_s[base + hist - CONV_HIST:base + hist, :] = new_hist
        if n_sub > 1:
            qk_s[g, j * seq_len:(j + 1) * seq_len, :] = acc
        yield
    qk = jax.nn.silu(qk_s[g] if n_sub > 1 else acc)
    q_all = qk[:, :MLSTM_WIDTH] * (MLSTM_HEAD_DIM ** -0.5)
    k_all = qk[:, MLSTM_WIDTH:]
    v_all = v_ref[g]
    yield

    def seq_of(idx):
        s = jnp.zeros(idx.shape, F32)
        for j in range(1, n_sub):
            s = s + jnp.where(idx >= j * seq_len, 1.0, 0.0)
        return s

    row_seq = seq_of(lax.broadcasted_iota(jnp.int32, (chunk, 1), 0))
    col_seq = seq_of(lax.broadcasted_iota(jnp.int32, (1, chunk), 1))
    t_id = lax.broadcasted_iota(jnp.int32, (chunk, chunk), 0)
    s_id = lax.broadcasted_iota(jnp.int32, (chunk, chunk), 1)
    same = jnp.where(row_seq == col_seq, 1.0, 0.0) if n_sub > 1 else jnp.ones((chunk, chunk), F32)
    tril = jnp.where(s_id <= t_id, same, 0.0)
    triu = jnp.where(t_id <= s_id, same, 0.0)
    causal = tril > 0.5
    pick = (lax.broadcasted_iota(jnp.int32, (SUBLANES, LANES), 0)
            == lax.broadcasted_iota(jnp.int32, (SUBLANES, LANES), 1)).astype(F32)

    li_col = gc_ref[g] + bc_ref[...]
    li_row = _dot_nt(pick, li_col, precision=HIGHEST)
    lf_col = jax.nn.log_sigmoid(li_col)
    yield
    lf_row = jax.nn.log_sigmoid(li_row)
    b_col = jnp.dot(tril, lf_col, preferred_element_type=F32, precision=HIGHEST)
    b_row = jnp.dot(lf_row, triu, preferred_element_type=F32, precision=HIGHEST)
    yield

    def per_row(vals):
        out = vals[0]
        for j in range(1, n_sub):
            out = jnp.where(row_seq == j, vals[j], out)
        return out

    m_new_parts = [[] for _ in range(n_sub)]
    for h in range(MLSTM_HEADS):
        hs = slice(h * MLSTM_HEAD_DIM, (h + 1) * MLSTM_HEAD_DIM)
        bc = b_col[:, MLSTM_HEADS + h:MLSTM_HEADS + h + 1]
        br = b_row[MLSTM_HEADS + h:MLSTM_HEADS + h + 1, :]
        lic = li_col[:, h:h + 1]
        lir = li_row[h:h + 1, :]
        m_prev = [m_in[s][:, h:h + 1] for s in slots]
        c_old = [c_in[s, h] for s in slots]
        n_old = [n_in[s, h:h + 1, :] for s in slots]
        qf = q_all[:, hs]
        kf = k_all[:, hs]
        qb = qf.astype(BF16)
        vb = v_all[:, hs].astype(BF16)

        dmat = jnp.where(causal, bc - br + lir, -jnp.inf)
        inter = bc + per_row(m_prev)
        m_t = jnp.maximum(inter, jnp.max(dmat, axis=-1, keepdims=True))
        s_qk = _dot_nt(qb, kf.astype(BF16))
        q_c = per_row([_dot(qb, c.astype(BF16)) for c in c_old])
        yield
        w = jnp.exp(dmat - m_t)
        scores = s_qk * w
        a = jnp.exp(inter - m_t)
        num = _dot(scores.astype(BF16), vb) + a * q_c
        nq = (jnp.sum(scores, axis=-1, keepdims=True)
              + a * jnp.sum(qf * per_row(n_old), axis=-1, keepdims=True))
        yield
        hout = num / jnp.maximum(jnp.abs(nq), jnp.exp(-m_t))
        hout = hout * lax.rsqrt(jnp.mean(hout * hout, axis=-1, keepdims=True) + EPS)
        hm_ref[g, :, hs] = hout * gm_ref[:, hs]
        yield

        for j, slot in enumerate(slots):
            last = (j + 1) * seq_len - 1
            b_last = br[:, last:last + 1]
            g_row = b_last - br + lir
            if n_sub > 1:
                g_row = jnp.where(col_seq == j, g_row, -jnp.inf)
            m_new = jnp.maximum(b_last + m_prev[j], jnp.max(g_row, axis=-1, keepdims=True))
            wk = jnp.exp(b_last - bc + lic - m_new)
            if n_sub > 1:
                wk = jnp.where(row_seq == j, wk, 0.0)
            decay = jnp.exp(b_last + m_prev[j] - m_new)
            kw = kf * wk
            c_ref[slot, h] = decay * c_old[j] + _dot_tn(kw.astype(BF16), vb)
            n_ref[slot, h:h + 1, :] = decay * n_old[j] + jnp.sum(kw, axis=0, keepdims=True)
            m_new_parts[j].append(m_new)
            yield
    for j, slot in enumerate(slots):
        m_ref[slot] = jnp.concatenate(m_new_parts[j], axis=1)


def _mlstm_body(*refs, n_sub, seq_len, has_state, n_prev, single_chunk, par):
    qk_ref, v_ref, gc_ref, cw_ref, cb_ref, bc_ref, gm_ref = refs[:7]
    refs = refs[7:]
    c0_ref = n0_ref = m0_ref = conv0_ref = None
    if has_state:
        c0_ref, n0_ref, m0_ref, conv0_ref = refs[:4]
        refs = refs[4:]
    hm_ref, c_ref, n_ref, m_ref, conv_ref = refs[n_prev:n_prev + 5]
    scratch = refs[n_prev + 5:]
    xp_s = scratch[0]
    qk_s = scratch[1] if n_sub > 1 else None

    from_input = has_state and single_chunk
    c_in, n_in, m_in = (c0_ref, n0_ref, m0_ref) if from_input else (c_ref, n_ref, m_ref)

    @pl.when(pl.program_id(1) == 0)
    def _():
        _mlstm_init(c_ref, n_ref, m_ref, xp_s, None if from_input else (c0_ref, n0_ref, m0_ref), conv0_ref,
                    state_from_input=from_input)

    tiles = [_mlstm_tile(g, qk_ref, v_ref, gc_ref, cw_ref, cb_ref, bc_ref, gm_ref,
                         c_in, n_in, m_in, hm_ref, c_ref, n_ref, m_ref, conv_ref, xp_s, qk_s,
                         n_sub=n_sub, seq_len=seq_len, carry_hist=not single_chunk) for g in range(par)]
    order = itertools.zip_longest(*tiles) if single_chunk else itertools.chain(*tiles)
    for _ in order:
        pass


def _mlstm_init(c_ref, n_ref, m_ref, xp_s, state0, conv0_ref, *, state_from_input):
    n_slots = c_ref.shape[0]
    region = xp_s.shape[0] // n_slots
    for slot in range(n_slots):
        xp_s[slot * region:slot * region + SUBLANES, :] = jnp.zeros((SUBLANES, 2 * MLSTM_WIDTH), F32)
        if conv0_ref is not None:
            xp_s[slot * region + SUBLANES - CONV_HIST:slot * region + SUBLANES, :] = conv0_ref[slot]
    if state_from_input:
        return
    for ref, ref0 in zip((c_ref, n_ref, m_ref), state0):
        ref[...] = jnp.zeros_like(ref) if ref0 is None else ref0[...]


def _mlstm(qk, v, gc, cw, cb, bias_c, gm, state, prev, layer, depth, n_tiles, n_sub, seq_len, n_chunk, par):
    assert n_sub == 1 or n_chunk == 1
    assert n_tiles % par == 0
    hh, dh = MLSTM_HEADS, MLSTM_HEAD_DIM
    chunk = n_sub * seq_len
    n_slots = par * n_sub
    n_all = n_tiles * n_sub
    has_state = state is not None
    tile3 = lambda t: t.reshape(n_tiles, n_chunk * chunk, t.shape[-1])
    act_spec = lambda w: pl.BlockSpec((par, chunk, w), lambda b, c: (b, c, 0))
    st5 = lambda b, c: (layer, b, 0, 0, 0)
    st4 = lambda b, c: (layer, b, 0, 0)
    state_specs = [pl.BlockSpec((None, n_slots, hh, dh, dh), st5),
                   pl.BlockSpec((None, n_slots, hh, dh), st4),
                   pl.BlockSpec((None, n_slots, 1, hh), st4),
                   pl.BlockSpec((None, n_slots, CONV_HIST, 2 * MLSTM_WIDTH), st4)]
    in_specs = [act_spec(2 * MLSTM_WIDTH), act_spec(MLSTM_WIDTH), act_spec(LANES),
                _layer_spec((CONV_WIDTH, 2 * MLSTM_WIDTH), layer), _layer_spec((1, 2 * MLSTM_WIDTH), layer),
                _layer_spec((1, LANES), layer), _layer_spec((1, MLSTM_WIDTH), layer)]
    args = [tile3(qk), tile3(v), tile3(gc), cw, cb, bias_c, gm]
    if has_state:
        in_specs += state_specs
        args += list(state)
    aliases = {len(args) + k: 1 + k for k in range(len(prev))}
    in_specs += [_ANY] * len(prev)
    args += list(prev)
    out_shape = [jax.ShapeDtypeStruct((n_tiles, n_chunk * chunk, MLSTM_WIDTH), F32),
                 jax.ShapeDtypeStruct((depth, n_all, hh, dh, dh), F32),
                 jax.ShapeDtypeStruct((depth, n_all, hh, dh), F32),
                 jax.ShapeDtypeStruct((depth, n_all, 1, hh), F32),
                 jax.ShapeDtypeStruct((depth, n_all, CONV_HIST, 2 * MLSTM_WIDTH), F32)]
    region = SUBLANES + -(-seq_len // SUBLANES) * SUBLANES
    scratch = [pltpu.VMEM((n_slots * region, 2 * MLSTM_WIDTH), F32)]
    if n_sub > 1:
        scratch.append(pltpu.VMEM((par, chunk, 2 * MLSTM_WIDTH), F32))
    return pl.pallas_call(
        functools.partial(_mlstm_body, n_sub=n_sub, seq_len=seq_len, has_state=has_state,
                          n_prev=len(prev), single_chunk=n_chunk == 1, par=par),
        grid=(n_tiles // par, n_chunk),
        in_specs=in_specs,
        out_specs=[act_spec(MLSTM_WIDTH)] + state_specs,
        out_shape=out_shape,
        scratch_shapes=scratch,
        input_output_aliases=aliases,
        compiler_params=_params(("parallel", "arbitrary")),
        name="mlstm",
    )(*args)


def _post_body(x_ref, y_ref, hm_ref, g_ref, wo_ref, wgs_ref, wgm_ref, wglu_ref, wsu_ref, wmu_ref,
               wout_ref, o_ref):
    x = x_ref[...]
    h = _rms(x, g_ref[...]).astype(BF16)
    ys = jax.nn.gelu(y_ref[...])
    ys = ys * jax.nn.sigmoid(_dot(ys.astype(BF16), wglu_ref[...]))
    hm = hm_ref[...] * jax.nn.sigmoid(_dot(h, wo_ref[...]))
    merged = (jax.nn.sigmoid(_dot(h, wgs_ref[...])) * _dot(ys.astype(BF16), wsu_ref[...])
              + jax.nn.sigmoid(_dot(h, wgm_ref[...])) * _dot(hm.astype(BF16), wmu_ref[...]))
    o_ref[...] = x + _dot(merged.astype(BF16), wout_ref[...])


def _post(x, y, hm, g, wo, wgs, wgm, wglu, wsu, wmu, wout, layer):
    rows = x.shape[0]
    tm = min(ROW_TILE, rows)
    rb = lambda i: (i, 0)
    sq = _layer_spec((D_MODEL, D_MODEL), layer)
    return pl.pallas_call(
        _post_body,
        grid=(rows // tm,),
        in_specs=[pl.BlockSpec((tm, D_MODEL), rb), pl.BlockSpec((tm, S5_WIDTH), rb),
                  pl.BlockSpec((tm, MLSTM_WIDTH), rb),
                  _layer_spec((1, D_MODEL), layer), sq, sq, sq,
                  _layer_spec((S5_WIDTH, S5_WIDTH), layer), _layer_spec((S5_WIDTH, D_MODEL), layer), sq, sq],
        out_specs=pl.BlockSpec((tm, D_MODEL), rb),
        out_shape=jax.ShapeDtypeStruct((rows, D_MODEL), F32),
        compiler_params=_params(("parallel",)),
        name="post",
    )(x, y, hm, g, wo, wgs, wgm, wglu, wsu, wmu, wout)


def _block_diag(t):
    depth, nb, gpb, a, c = t.shape
    eye = jnp.eye(gpb, dtype=t.dtype)
    return jnp.einsum("ljgac,gk->ljgakc", t, eye).reshape(depth, nb, gpb * a, gpb * c)


def kernel(x_prompt, x_sample, state_s5_re, state_s5_im, state_mlstm_c, state_mlstm_n, state_mlstm_m, state_conv, p_prompt, p_sample, g_ffn1, w1_gate, w1_up, w1_down, g_mix, w_in, s5_lambda_re, s5_lambda_im, s5_log_dt, s5_b_re, s5_b_im, s5_c_re, s5_c_im, s5_d, s5_w_glu, w_s5_up, conv_w, conv_b, b_igate, b_fgate, g_mhead, w_m_up, w_out, g_ffn2, w2_gate, w2_up, w2_down, g_ple, w_ple, w_ple_gate, g_final):
    depth = w_in.shape[0]
    bp, sp, _ = x_prompt.shape
    bs, ss, _ = x_sample.shape
    hh = MLSTM_HEADS
    gpb = S5_GROUPS // S5_BLOCKS

    bf = lambda w: w.astype(BF16)
    cuts = [0, S5_WIDTH, S5_WIDTH + 2 * MLSTM_WIDTH, S5_WIDTH + 3 * MLSTM_WIDTH, S5_WIDTH + 4 * MLSTM_WIDTH]
    c_gate = cuts[4] + 2 * MLSTM_HEADS
    w_u = bf(w_in[:, :, cuts[0]:cuts[1]])
    w_qk = bf(w_in[:, :, cuts[1]:cuts[2]])
    w_v = bf(w_in[:, :, cuts[2]:cuts[3]])
    w_o = bf(w_in[:, :, cuts[3]:cuts[4]])
    w_gc = bf(jnp.pad(w_in[:, :, cuts[4]:c_gate], ((0, 0), (0, 0), (0, LANES - 2 * MLSTM_HEADS))))
    w_gs = bf(w_in[:, :, c_gate:c_gate + D_MODEL])
    w_gm = bf(w_in[:, :, c_gate + D_MODEL:])
    w1g, w1u, w1d = bf(w1_gate), bf(w1_up), bf(w1_down)
    w2g, w2u, w2d = bf(w2_gate), bf(w2_up), bf(w2_down)
    wglu, wsu, wmu, wout = bf(s5_w_glu), bf(w_s5_up), bf(w_m_up), bf(w_out)
    wple, wpg = bf(w_ple), bf(w_ple_gate)
    row = lambda g: g.reshape(depth, 1, -1)
    gf1, gmx, gf2, gpl, gmh, s5d, cvb = (row(g_ffn1), row(g_mix), row(g_ffn2), row(g_ple),
                                          row(g_mhead), row(s5_d), row(conv_b))
    gfin = g_final.reshape(1, D_MODEL)
    bias_c = jnp.pad(jnp.concatenate([b_igate, b_fgate], axis=1),
                     ((0, 0), (0, LANES - 2 * MLSTM_HEADS))).reshape(depth, 1, LANES)

    lanes3 = lambda t: t.reshape(depth, 1, S5_LANES)
    ldt = jnp.broadcast_to(s5_log_dt[:, :, None], (depth, S5_GROUPS, S5_STATE))
    to_cols = lambda t: jnp.transpose(t, (0, 3, 1, 2)).reshape(depth, S5_GROUP, S5_LANES)
    a_re, a_im, bb_re, bb_im = _s5_params(lanes3(s5_lambda_re), lanes3(s5_lambda_im), lanes3(ldt),
                                          to_cols(s5_b_re), to_cols(s5_b_im))
    to_blk = lambda t: jnp.transpose(t.reshape(depth, S5_GROUP, S5_BLOCKS, gpb, S5_STATE), (0, 2, 3, 1, 4))
    bmat = bf(jnp.concatenate([_block_diag(to_blk(bb_re)), _block_diag(to_blk(bb_im))], axis=-1))
    c_blk = lambda t: jnp.transpose(t.reshape(depth, S5_BLOCKS, gpb, S5_GROUP, S5_STATE), (0, 1, 2, 4, 3))
    cmat_re = bf(_block_diag(c_blk(s5_c_re)))
    cmat_im = bf(_block_diag(c_blk(s5_c_im)))
    s5_time_p = S5_ROWS // bp
    perm_p = _time_major_perm(bp, s5_time_p)
    perm_s = _time_major_perm(bs, ss)

    xp = x_prompt.reshape(bp * sp, D_MODEL)
    xs = x_sample.reshape(bs * ss, D_MODEL)
    pp = p_prompt.reshape(depth, bp * sp, D_PLE)
    ps = p_sample.reshape(depth, bs * ss, D_PLE)
    to_blocks = lambda t: jnp.swapaxes(t.reshape(depth, -1, S5_BLOCKS, S5_BLOCK_ST), 1, 2)
    from_blocks = lambda t: jnp.swapaxes(t, 1, 2).reshape(depth, -1, S5_GROUPS, S5_STATE)
    a_re = a_re.reshape(depth, S5_BLOCKS, 1, S5_BLOCK_ST)
    a_im = a_im.reshape(depth, S5_BLOCKS, 1, S5_BLOCK_ST)
    s5_state = (to_blocks(state_s5_re), to_blocks(state_s5_im))
    lstm_state = (state_mlstm_c, state_mlstm_n, state_mlstm_m.reshape(depth, bs, 1, hh), state_conv)

    s5_out_p, s5_out_s, lstm_out_p, lstm_out_s = [], [], [], []
    for i in range(depth):
        final = i == depth - 1
        ffn1_w = (gf1, w1g, w1u, w1d)
        proj_w = (gmx, w_u, w_qk, w_v, w_gc)
        s5_w = (a_re, a_im, bmat, cmat_re, cmat_im, s5d)
        lstm_w = (conv_w, cvb, bias_c, gmh)
        mix_w = (gmx, w_o, w_gs, w_gm, wglu, wsu, wmu, wout)
        ple_w = (gf2, w2g, w2u, w2d, gpl, wple, wpg, gfin)

        xp = _ffn(xp, *ffn1_w, layer=i)
        u, qk, v, gc = _inproj(xp, *proj_w, layer=i, v_dtype=BF16)
        y, s5_re, s5_im = _s5(u, perm_p, perm_p.T, *s5_w, None, layer=i, n_batch=bp, n_time=s5_time_p)
        s5_out_p.append((s5_re, s5_im))
        hm, *lstm_out_p = _mlstm(qk, v, gc, *lstm_w, None, lstm_out_p, layer=i, depth=depth,
                                 n_tiles=bp, n_sub=1, seq_len=MLSTM_CHUNK, n_chunk=sp // MLSTM_CHUNK,
                                 par=PROMPT_TILES_PER_STEP)
        xp = _post(xp, y.reshape(bp * sp, S5_WIDTH), hm.reshape(bp * sp, MLSTM_WIDTH), *mix_w, layer=i)
        xp = _ffn_ple(xp, pp, *ple_w, layer=i, final=final)

        xs = _ffn(xs, *ffn1_w, layer=i)
        u, qk, v, gc = _inproj(xs, *proj_w, layer=i, v_dtype=F32)
        y, s5_re, s5_im = _s5(u, perm_s, perm_s.T, *s5_w, s5_state, layer=i, n_batch=bs, n_time=ss)
        s5_out_s.append((s5_re, s5_im))
        hm, *lstm_out_s = _mlstm(qk, v, gc, *lstm_w, lstm_state, lstm_out_s, layer=i, depth=depth,
                                 n_tiles=bs // SAMPLE_SEQS_PER_TILE, n_sub=SAMPLE_SEQS_PER_TILE,
                                 seq_len=ss, n_chunk=1, par=SAMPLE_TILES_PER_STEP)
        xs = _post(xs, y.reshape(bs * ss, S5_WIDTH), hm.reshape(bs * ss, MLSTM_WIDTH), *mix_w, layer=i)
        xs = _ffn_ple(xs, ps, *ple_w, layer=i, final=final)

    def states(n, s5_out, lstm_out):
        c_n, n_n, m_n, conv_n = lstm_out
        s5_re, s5_im = (jnp.stack(parts) for parts in zip(*s5_out))
        return (from_blocks(s5_re), from_blocks(s5_im), c_n, n_n, m_n.reshape(depth, n, hh), conv_n)

    return ((xp.reshape(bp, sp, D_MODEL), xs.reshape(bs, ss, D_MODEL))
            + states(bp, s5_out_p, lstm_out_p) + states(bs, s5_out_s, lstm_out_s))
```

```python
import functools
import itertools

import jax
import jax.numpy as jnp
from jax import lax
from jax.experimental import pallas as pl
from jax.experimental.pallas import tpu as pltpu

D_MODEL = 1024
D_PLE = 256
D_FF = 2048
S5_WIDTH = 512
S5_GROUP = 16
S5_GROUPS = S5_WIDTH // S5_GROUP
S5_STATE = 64
S5_LANES = S5_GROUPS * S5_STATE
MLSTM_WIDTH = 1024
MLSTM_HEADS = 4
MLSTM_HEAD_DIM = MLSTM_WIDTH // MLSTM_HEADS
MLSTM_CHUNK = 128
CONV_WIDTH = 4
CONV_HIST = CONV_WIDTH - 1
EPS = 1e-6

F32 = jnp.float32
BF16 = jnp.bfloat16
HIGHEST = lax.Precision.HIGHEST

SUBLANES = 8
LANES = 128
VMEM_LIMIT_BYTES = 56 * 1024 * 1024

ROW_TILE = 1024
S5_ROWS = 512
S5_BLOCKS = 4
S5_BLOCK_IN = S5_WIDTH // S5_BLOCKS
S5_BLOCK_ST = S5_LANES // S5_BLOCKS
SAMPLE_SEQS_PER_TILE = 2
PROMPT_TILES_PER_STEP = 4
SAMPLE_TILES_PER_STEP = 4


def _params(sem):
    return pltpu.CompilerParams(dimension_semantics=sem, vmem_limit_bytes=VMEM_LIMIT_BYTES)


def _layer_spec(shape, layer):
    nd = len(shape)
    return pl.BlockSpec((None,) + tuple(shape), lambda *_: (layer,) + (0,) * nd,
                        pipeline_mode=pl.Buffered(1))


def _const_spec(shape):
    nd = len(shape)
    return pl.BlockSpec(tuple(shape), lambda *_: (0,) * nd, pipeline_mode=pl.Buffered(1))


_ANY = pl.BlockSpec(memory_space=pl.ANY)


def _rms(x, g):
    return x * lax.rsqrt(jnp.mean(x * x, axis=-1, keepdims=True) + EPS) * g


def _dot(a, b):
    return jnp.dot(a, b, preferred_element_type=F32)


def _dot_nt(a, b, **kw):
    return lax.dot_general(a, b, (((1,), (1,)), ((), ())), preferred_element_type=F32, **kw)


def _dot_tn(a, b):
    return lax.dot_general(a, b, (((0,), (0,)), ((), ())), preferred_element_type=F32)


def _ffn_body(x_ref, g_ref, wg_ref, wu_ref, wd_ref, o_ref):
    x = x_ref[...]
    h = _rms(x, g_ref[...]).astype(BF16)
    z = (jax.nn.silu(_dot(h, wg_ref[...])) * _dot(h, wu_ref[...])).astype(BF16)
    o_ref[...] = x + 0.5 * _dot(z, wd_ref[...])


def _ffn_ple_body(x_ref, p_ref, g_ref, wg_ref, wu_ref, wd_ref, gp_ref, wp_ref, wpg_ref, gf_ref, o_ref,
                  *, final):
    x = x_ref[...]
    h = _rms(x, g_ref[...]).astype(BF16)
    z = (jax.nn.silu(_dot(h, wg_ref[...])) * _dot(h, wu_ref[...])).astype(BF16)
    x = x + 0.5 * _dot(z, wd_ref[...])
    gate = jax.nn.sigmoid(_dot(_rms(x, gp_ref[...]).astype(BF16), wpg_ref[...]))
    x = x + _dot(p_ref[...].astype(BF16), wp_ref[...]) * gate
    if final:
        x = _rms(x, gf_ref[...])
    o_ref[...] = x


def _ffn(x, g, wg, wu, wd, layer):
    rows = x.shape[0]
    tm = min(ROW_TILE, rows)
    row_spec = pl.BlockSpec((tm, D_MODEL), lambda i: (i, 0))
    return pl.pallas_call(
        _ffn_body,
        grid=(rows // tm,),
        in_specs=[row_spec, _layer_spec((1, D_MODEL), layer), _layer_spec((D_MODEL, D_FF), layer),
                  _layer_spec((D_MODEL, D_FF), layer), _layer_spec((D_FF, D_MODEL), layer)],
        out_specs=row_spec,
        out_shape=jax.ShapeDtypeStruct((rows, D_MODEL), F32),
        compiler_params=_params(("parallel",)),
        name="ffn",
    )(x, g, wg, wu, wd)


def _ffn_ple(x, p, g, wg, wu, wd, gp, wp, wpg, gf, layer, final):
    rows = x.shape[0]
    tm = min(ROW_TILE, rows)
    row_spec = pl.BlockSpec((tm, D_MODEL), lambda i: (i, 0))
    return pl.pallas_call(
        functools.partial(_ffn_ple_body, final=final),
        grid=(rows // tm,),
        in_specs=[row_spec, pl.BlockSpec((None, tm, D_PLE), lambda i: (layer, i, 0)),
                  _layer_spec((1, D_MODEL), layer), _layer_spec((D_MODEL, D_FF), layer),
                  _layer_spec((D_MODEL, D_FF), layer), _layer_spec((D_FF, D_MODEL), layer),
                  _layer_spec((1, D_MODEL), layer), _layer_spec((D_PLE, D_MODEL), layer),
                  _layer_spec((D_MODEL, D_MODEL), layer), _const_spec((1, D_MODEL))],
        out_specs=row_spec,
        out_shape=jax.ShapeDtypeStruct((rows, D_MODEL), F32),
        compiler_params=_params(("parallel",)),
        name="ffn_ple",
    )(x, p, g, wg, wu, wd, gp, wp, wpg, gf)


def _inproj_body(x_ref, g_ref, wu_ref, wqk_ref, wv_ref, wgc_ref, u_ref, qk_ref, v_ref, gc_ref):
    h = _rms(x_ref[...], g_ref[...]).astype(BF16)
    u_ref[...] = _dot(h, wu_ref[...])
    qk_ref[...] = _dot(h, wqk_ref[...])
    v_ref[...] = _dot(h, wv_ref[...]).astype(v_ref.dtype)
    gc_ref[...] = _dot(h, wgc_ref[...])


def _inproj(x, g, wu, wqk, wv, wgc, layer, v_dtype):
    rows = x.shape[0]
    tm = min(ROW_TILE, rows)
    rb = lambda i: (i, 0)
    widths = (S5_WIDTH, 2 * MLSTM_WIDTH, MLSTM_WIDTH, LANES)
    dtypes = (F32, F32, v_dtype, F32)
    return pl.pallas_call(
        _inproj_body,
        grid=(rows // tm,),
        in_specs=[pl.BlockSpec((tm, D_MODEL), rb), _layer_spec((1, D_MODEL), layer)]
        + [_layer_spec((D_MODEL, w), layer) for w in widths],
        out_specs=[pl.BlockSpec((tm, w), rb) for w in widths],
        out_shape=[jax.ShapeDtypeStruct((rows, w), dt) for w, dt in zip(widths, dtypes)],
        compiler_params=_params(("parallel",)),
        name="inproj",
    )(x, g, wu, wqk, wv, wgc)


def _s5_param_body(lre_ref, lim_ref, ldt_ref, bre_ref, bim_ref, are_ref, aim_ref, bbre_ref, bbim_ref):
    lre = lre_ref[0]
    lim = lim_ref[0]
    dt = jnp.exp(ldt_ref[0])
    mag = jnp.exp(lre * dt)
    a_re = mag * jnp.cos(lim * dt)
    a_im = mag * jnp.sin(lim * dt)
    den = lre * lre + lim * lim
    pr = a_re - 1.0
    w_re = (pr * lre + a_im * lim) / den
    w_im = (a_im * lre - pr * lim) / den
    are_ref[0] = a_re
    aim_ref[0] = a_im
    bbre_ref[0] = w_re * bre_ref[0] - w_im * bim_ref[0]
    bbim_ref[0] = w_re * bim_ref[0] + w_im * bre_ref[0]


def _s5_params(lre, lim, ldt, bre, bim):
    depth = lre.shape[0]
    vec = pl.BlockSpec((1, 1, S5_LANES), lambda i: (i, 0, 0))
    mat = pl.BlockSpec((1, S5_GROUP, S5_LANES), lambda i: (i, 0, 0))
    return pl.pallas_call(
        _s5_param_body,
        grid=(depth,),
        in_specs=[vec, vec, vec, mat, mat],
        out_specs=[vec, vec, mat, mat],
        out_shape=[jax.ShapeDtypeStruct((depth, 1, S5_LANES), F32)] * 2
        + [jax.ShapeDtypeStruct((depth, S5_GROUP, S5_LANES), F32)] * 2,
        compiler_params=_params(("parallel",)),
        name="s5_params",
    )(lre, lim, ldt, bre, bim)


def _s5_body(*refs, n_batch, n_time, has_state):
    weights = refs[:9]
    refs = refs[9:]
    h0 = None
    if has_state:
        h0 = refs[:2]
        refs = refs[2:]
    sr_ref, si_ref = refs[1:3]

    @pl.when(pl.program_id(0) == 0)
    def _():
        if h0 is not None:
            sr_ref[...] = h0[0][...]
            si_ref[...] = h0[1][...]
        else:
            sr_ref[...] = jnp.zeros_like(sr_ref)
            si_ref[...] = jnp.zeros_like(si_ref)

    _s5_chunk(*weights, *refs, n_batch=n_batch, n_time=n_time)


def _s5_chunk(u_ref, perm_ref, permt_ref, are_ref, aim_ref, bm_ref, cre_ref, cim_ref, d_ref,
              y_ref, sr_ref, si_ref, ub_s, bur_s, bui_s, hr_s, hi_s, y_s, *, n_batch, n_time):
    rows = n_batch * n_time
    n_bt = n_batch // SUBLANES

    u = u_ref[...].reshape(rows, S5_WIDTH)
    ub = _dot(perm_ref[...], u.astype(BF16)).astype(BF16)
    for j in range(S5_BLOCKS):
        ub_s[j] = ub[:, j * S5_BLOCK_IN:(j + 1) * S5_BLOCK_IN]

    def project_in(j):
        r = _dot(ub_s[j], bm_ref[j])
        bur_s[j] = r[:, :S5_BLOCK_ST]
        bui_s[j] = r[:, S5_BLOCK_ST:]

    def scan(j):
        a_re = jnp.broadcast_to(are_ref[j], (SUBLANES, S5_BLOCK_ST))
        a_im = jnp.broadcast_to(aim_ref[j], (SUBLANES, S5_BLOCK_ST))

        def advance(h, row):
            h_re, h_im = h
            tile = slice(row, row + SUBLANES)
            return (a_re * h_re - a_im * h_im + bur_s[j, tile, :],
                    a_re * h_im + a_im * h_re + bui_s[j, tile, :])

        def emit(row, first, second):
            pair = slice(row, row + 2 * SUBLANES)
            hr_s[j, pair, :] = jnp.concatenate([first[0], second[0]], axis=0).astype(BF16)
            hi_s[j, pair, :] = jnp.concatenate([first[1], second[1]], axis=0).astype(BF16)

        if n_bt == 1:
            h = (sr_ref[j], si_ref[j])
            for t in range(0, n_time, 2):
                h1 = advance(h, t * SUBLANES)
                h = advance(h1, (t + 1) * SUBLANES)
                emit(t * SUBLANES, h1, h)
            sr_ref[j], si_ref[j] = h
        else:
            for bt in range(0, n_bt, 2):
                rows_a = slice(bt * SUBLANES, (bt + 1) * SUBLANES)
                rows_b = slice((bt + 1) * SUBLANES, (bt + 2) * SUBLANES)
                ha = (sr_ref[j, rows_a, :], si_ref[j, rows_a, :])
                hb = (sr_ref[j, rows_b, :], si_ref[j, rows_b, :])
                for t in range(n_time):
                    row = t * n_batch + bt * SUBLANES
                    ha = advance(ha, row)
                    hb = advance(hb, row + SUBLANES)
                    emit(row, ha, hb)
                sr_ref[j, rows_a, :], si_ref[j, rows_a, :] = ha
                sr_ref[j, rows_b, :], si_ref[j, rows_b, :] = hb

    def project_out(j):
        y_s[j] = _dot(hr_s[j], cre_ref[j]) - _dot(hi_s[j], cim_ref[j])

    project_in(0)

    def pipelined(j, carry):
        project_in(j + 1)
        scan(j)
        project_out(j)
        return carry

    lax.fori_loop(0, S5_BLOCKS - 1, pipelined, 0)
    scan(S5_BLOCKS - 1)
    project_out(S5_BLOCKS - 1)
    y = jnp.concatenate([y_s[j] for j in range(S5_BLOCKS)], axis=1)
    y_hi = y.astype(BF16)
    y_lo = (y - y_hi.astype(F32)).astype(BF16)
    permt = permt_ref[...]
    y = _dot(permt, y_hi) + _dot(permt, y_lo) + d_ref[...] * u
    y_ref[...] = y.reshape(y_ref.shape)


def _s5(u, perm, permt, a_re, a_im, bm, cre, cim, d, h0, layer, n_batch, n_time):
    seq = u.shape[0] // n_batch
    rows = n_batch * n_time
    has_state = h0 is not None
    assert n_batch % (2 * SUBLANES) == 0 or (n_batch == SUBLANES and n_time % 2 == 0)
    lead, blk_rows = (n_batch, n_time) if seq > n_time else (1, rows)
    u = u.reshape(lead, u.shape[0] // lead, S5_WIDTH)
    u_spec = pl.BlockSpec((lead, blk_rows, S5_WIDTH), lambda c: (0, c, 0))
    state_in_spec = pl.BlockSpec((None, S5_BLOCKS, n_batch, S5_BLOCK_ST), lambda c: (layer, 0, 0, 0))
    state_spec = pl.BlockSpec((S5_BLOCKS, n_batch, S5_BLOCK_ST), lambda c: (0, 0, 0))
    state_shape = jax.ShapeDtypeStruct((S5_BLOCKS, n_batch, S5_BLOCK_ST), F32)
    in_specs = [u_spec,
                _const_spec((rows, rows)), _const_spec((rows, rows)),
                _layer_spec((S5_BLOCKS, 1, S5_BLOCK_ST), layer), _layer_spec((S5_BLOCKS, 1, S5_BLOCK_ST), layer),
                _layer_spec((S5_BLOCKS, S5_BLOCK_IN, 2 * S5_BLOCK_ST), layer),
                _layer_spec((S5_BLOCKS, S5_BLOCK_ST, S5_BLOCK_IN), layer),
                _layer_spec((S5_BLOCKS, S5_BLOCK_ST, S5_BLOCK_IN), layer),
                _layer_spec((1, S5_WIDTH), layer)]
    args = [u, perm, permt, a_re, a_im, bm, cre, cim, d]
    if has_state:
        in_specs += [state_in_spec, state_in_spec]
        args += list(h0)
    return pl.pallas_call(
        functools.partial(_s5_body, n_batch=n_batch, n_time=n_time, has_state=has_state),
        grid=(seq // n_time,),
        in_specs=in_specs,
        out_specs=[u_spec, state_spec, state_spec],
        out_shape=[jax.ShapeDtypeStruct(u.shape, F32), state_shape, state_shape],
        scratch_shapes=[pltpu.VMEM((S5_BLOCKS, rows, S5_BLOCK_IN), BF16),
                        pltpu.VMEM((S5_BLOCKS, rows, S5_BLOCK_ST), F32),
                        pltpu.VMEM((S5_BLOCKS, rows, S5_BLOCK_ST), F32),
                        pltpu.VMEM((S5_BLOCKS, rows, S5_BLOCK_ST), BF16),
                        pltpu.VMEM((S5_BLOCKS, rows, S5_BLOCK_ST), BF16),
                        pltpu.VMEM((S5_BLOCKS, rows, S5_BLOCK_IN), F32)],
        compiler_params=_params(("arbitrary",)),
        name="s5",
    )(*args)


def _time_major_perm(n_batch, n_time):
    r = jnp.arange(n_batch * n_time)
    src = (r % n_batch) * n_time + r // n_batch
    return (src[:, None] == r[None, :]).astype(BF16)


def _mlstm_tile(g, qk_ref, v_ref, gc_ref, cw_ref, cb_ref, bc_ref, gm_ref,
                c_in, n_in, m_in, hm_ref, c_ref, n_ref, m_ref, conv_ref, xp_s, qk_s,
                *, n_sub, seq_len, carry_hist):
    chunk = n_sub * seq_len
    region = SUBLANES + -(-seq_len // SUBLANES) * SUBLANES
    hist = SUBLANES
    slots = [g * n_sub + j for j in range(n_sub)]

    for j, slot in enumerate(slots):
        base = slot * region
        x_raw = qk_ref[g, j * seq_len:(j + 1) * seq_len, :]
        xp_s[base + hist:base + hist + seq_len, :] = x_raw
        if seq_len % SUBLANES == 0:
            x3 = x_raw.reshape(seq_len // SUBLANES, SUBLANES, 2 * MLSTM_WIDTH)
            prev = xp_s[base:base + hist, :]
            sub = lax.broadcasted_iota(jnp.int32, (1, SUBLANES, 1), 1)
            acc = cb_ref[...] + x3 * cw_ref[CONV_HIST:CONV_WIDTH, :]
            for k in range(1, CONV_WIDTH):
                rot = pltpu.roll(x3, k, axis=1)
                before = jnp.concatenate([pltpu.roll(prev, k, axis=0)[None], rot[:-1]], axis=0)
                acc = acc + jnp.where(sub < k, before, rot) * cw_ref[CONV_HIST - k:CONV_WIDTH - k, :]
            acc = acc.reshape(seq_len, 2 * MLSTM_WIDTH)
        else:
            acc = jnp.broadcast_to(cb_ref[...], (seq_len, 2 * MLSTM_WIDTH))
            for i in range(CONV_WIDTH):
                off = base + hist - CONV_HIST + i
                acc = acc + xp_s[off:off + seq_len, :] * cw_ref[i:i + 1, :]
        new_hist = xp_s[base + hist + seq_len - CONV_HIST:base + hist + seq_len, :]
        conv_ref[slot] = new_hist
        if carry_hist:
            xp_s[base + hist - CONV_HIST:base + hist, :] = new_hist
        if n_sub > 1:
            qk_s[g, j * seq_len:(j + 1) * seq_len, :] = acc
        yield
    qk = jax.nn.silu(qk_s[g] if n_sub > 1 else acc)
    q_all = qk[:, :MLSTM_WIDTH] * (MLSTM_HEAD_DIM ** -0.5)
    k_all = qk[:, MLSTM_WIDTH:]
    v_all = v_ref[g]
    yield

    def seq_of(idx):
        s = jnp.zeros(idx.shape, F32)
        for j in range(1, n_sub):
            s = s + jnp.where(idx >= j * seq_len, 1.0, 0.0)
        return s

    row_seq = seq_of(lax.broadcasted_iota(jnp.int32, (chunk, 1), 0))
    col_seq = seq_of(lax.broadcasted_iota(jnp.int32, (1, chunk), 1))
    t_id = lax.broadcasted_iota(jnp.int32, (chunk, chunk), 0)
    s_id = lax.broadcasted_iota(jnp.int32, (chunk, chunk), 1)
    same = jnp.where(row_seq == col_seq, 1.0, 0.0) if n_sub > 1 else jnp.ones((chunk, chunk), F32)
    tril = jnp.where(s_id <= t_id, same, 0.0)
    triu = jnp.where(t_id <= s_id, same, 0.0)
    causal = tril > 0.5
    pick = (lax.broadcasted_iota(jnp.int32, (SUBLANES, LANES), 0)
            == lax.broadcasted_iota(jnp.int32, (SUBLANES, LANES), 1)).astype(F32)

    li_col = gc_ref[g] + bc_ref[...]
    li_row = _dot_nt(pick, li_col, precision=HIGHEST)
    lf_col = jax.nn.log_sigmoid(li_col)
    yield
    lf_row = jax.nn.log_sigmoid(li_row)
    b_col = jnp.dot(tril, lf_col, preferred_element_type=F32, precision=HIGHEST)
    b_row = jnp.dot(lf_row, triu, preferred_element_type=F32, precision=HIGHEST)
    yield

    def per_row(vals):
        out = vals[0]
        for j in range(1, n_sub):
            out = jnp.where(row_seq == j, vals[j], out)
        return out

    m_new_parts = [[] for _ in range(n_sub)]
    for h in range(MLSTM_HEADS):
        hs = slice(h * MLSTM_HEAD_DIM, (h + 1) * MLSTM_HEAD_DIM)
        bc = b_col[:, MLSTM_HEADS + h:MLSTM_HEADS + h + 1]
        br = b_row[MLSTM_HEADS + h:MLSTM_HEADS + h + 1, :]
        lic = li_col[:, h:h + 1]
        lir = li_row[h:h + 1, :]
        m_prev = [m_in[s][:, h:h + 1] for s in slots]
        c_old = [c_in[s, h] for s in slots]
        n_old = [n_in[s, h:h + 1, :] for s in slots]
        qf = q_all[:, hs]
        kf = k_all[:, hs]
        qb = qf.astype(BF16)
        vb = v_all[:, hs].astype(BF16)

        dmat = jnp.where(causal, bc - br + lir, -jnp.inf)
        inter = bc + per_row(m_prev)
        m_t = jnp.maximum(inter, jnp.max(dmat, axis=-1, keepdims=True))
        s_qk = _dot_nt(qb, kf.astype(BF16))
        q_c = per_row([_dot(qb, c.astype(BF16)) for c in c_old])
        yield
        w = jnp.exp(dmat - m_t)
        scores = s_qk * w
        a = jnp.exp(inter - m_t)
        num = _dot(scores.astype(BF16), vb) + a * q_c
        nq = (jnp.sum(scores, axis=-1, keepdims=True)
              + a * jnp.sum(qf * per_row(n_old), axis=-1, keepdims=True))
        yield
        hout = num / jnp.maximum(jnp.abs(nq), jnp.exp(-m_t))
        hout = hout * lax.rsqrt(jnp.mean(hout * hout, axis=-1, keepdims=True) + EPS)
        hm_ref[g, :, hs] = hout * gm_ref[:, hs]
        yield

        for j, slot in enumerate(slots):
            last = (j + 1) * seq_len - 1
            b_last = br[:, last:last + 1]
            g_row = b_last - br + lir
            if n_sub > 1:
                g_row = jnp.where(col_seq == j, g_row, -jnp.inf)
            m_new = jnp.maximum(b_last + m_prev[j], jnp.max(g_row, axis=-1, keepdims=True))
            wk = jnp.exp(b_last - bc + lic - m_new)
            if n_sub > 1:
                wk = jnp.where(row_seq == j, wk, 0.0)
            decay = jnp.exp(b_last + m_prev[j] - m_new)
            kw = kf * wk
            c_ref[slot, h] = decay * c_old[j] + _dot_tn(kw.astype(BF16), vb)
            n_ref[slot, h:h + 1, :] = decay * n_old[j] + jnp.sum(kw, axis=0, keepdims=True)
            m_new_parts[j].append(m_new)
            yield
    for j, slot in enumerate(slots):
        m_ref[slot] = jnp.concatenate(m_new_parts[j], axis=1)


def _mlstm_body(*refs, n_sub, seq_len, has_state, n_prev, single_chunk, par):
    qk_ref, v_ref, gc_ref, cw_ref, cb_ref, bc_ref, gm_ref = refs[:7]
    refs = refs[7:]
    c0_ref = n0_ref = m0_ref = conv0_ref = None
    if has_state:
        c0_ref, n0_ref, m0_ref, conv0_ref = refs[:4]
        refs = refs[4:]
    hm_ref, c_ref, n_ref, m_ref, conv_ref = refs[n_prev:n_prev + 5]
    scratch = refs[n_prev + 5:]
    xp_s = scratch[0]
    qk_s = scratch[1] if n_sub > 1 else None

    from_input = has_state and single_chunk
    c_in, n_in, m_in = (c0_ref, n0_ref, m0_ref) if from_input else (c_ref, n_ref, m_ref)

    @pl.when(pl.program_id(1) == 0)
    def _():
        _mlstm_init(c_ref, n_ref, m_ref, xp_s, None if from_input else (c0_ref, n0_ref, m0_ref), conv0_ref,
                    state_from_input=from_input)

    tiles = [_mlstm_tile(g, qk_ref, v_ref, gc_ref, cw_ref, cb_ref, bc_ref, gm_ref,
                         c_in, n_in, m_in, hm_ref, c_ref, n_ref, m_ref, conv_ref, xp_s, qk_s,
                         n_sub=n_sub, seq_len=seq_len, carry_hist=not single_chunk) for g in range(par)]
    order = itertools.zip_longest(*tiles) if single_chunk else itertools.chain(*tiles)
    for _ in order:
        pass


def _mlstm_init(c_ref, n_ref, m_ref, xp_s, state0, conv0_ref, *, state_from_input):
    n_slots = c_ref.shape[0]
    region = xp_s.shape[0] // n_slots
    for slot in range(n_slots):
        xp_s[slot * region:slot * region + SUBLANES, :] = jnp.zeros((SUBLANES, 2 * MLSTM_WIDTH), F32)
        if conv0_ref is not None:
            xp_s[slot * region + SUBLANES - CONV_HIST:slot * region + SUBLANES, :] = conv0_ref[slot]
    if state_from_input:
        return
    for ref, ref0 in zip((c_ref, n_ref, m_ref), state0):
        ref[...] = jnp.zeros_like(ref) if ref0 is None else ref0[...]


def _mlstm(qk, v, gc, cw, cb, bias_c, gm, state, prev, layer, depth, n_tiles, n_sub, seq_len, n_chunk, par):
    assert n_sub == 1 or n_chunk == 1
    assert n_tiles % par == 0
    hh, dh = MLSTM_HEADS, MLSTM_HEAD_DIM
    chunk = n_sub * seq_len
    n_slots = par * n_sub
    n_all = n_tiles * n_sub
    has_state = state is not None
    tile3 = lambda t: t.reshape(n_tiles, n_chunk * chunk, t.shape[-1])
    act_spec = lambda w: pl.BlockSpec((par, chunk, w), lambda b, c: (b, c, 0))
    st5 = lambda b, c: (layer, b, 0, 0, 0)
    st4 = lambda b, c: (layer, b, 0, 0)
    state_specs = [pl.BlockSpec((None, n_slots, hh, dh, dh), st5),
                   pl.BlockSpec((None, n_slots, hh, dh), st4),
                   pl.BlockSpec((None, n_slots, 1, hh), st4),
                   pl.BlockSpec((None, n_slots, CONV_HIST, 2 * MLSTM_WIDTH), st4)]
    in_specs = [act_spec(2 * MLSTM_WIDTH), act_spec(MLSTM_WIDTH), act_spec(LANES),
                _layer_spec((CONV_WIDTH, 2 * MLSTM_WIDTH), layer), _layer_spec((1, 2 * MLSTM_WIDTH), layer),
                _layer_spec((1, LANES), layer), _layer_spec((1, MLSTM_WIDTH), layer)]
    args = [tile3(qk), tile3(v), tile3(gc), cw, cb, bias_c, gm]
    if has_state:
        in_specs += state_specs
        args += list(state)
    aliases = {len(args) + k: 1 + k for k in range(len(prev))}
    in_specs += [_ANY] * len(prev)
    args += list(prev)
    out_shape = [jax.ShapeDtypeStruct((n_tiles, n_chunk * chunk, MLSTM_WIDTH), F32),
                 jax.ShapeDtypeStruct((depth, n_all, hh, dh, dh), F32),
                 jax.ShapeDtypeStruct((depth, n_all, hh, dh), F32),
                 jax.ShapeDtypeStruct((depth, n_all, 1, hh), F32),
                 jax.ShapeDtypeStruct((depth, n_all, CONV_HIST, 2 * MLSTM_WIDTH), F32)]
    region = SUBLANES + -(-seq_len // SUBLANES) * SUBLANES
    scratch = [pltpu.VMEM((n_slots * region, 2 * MLSTM_WIDTH), F32)]
    if n_sub > 1:
        scratch.append(pltpu.VMEM((par, chunk, 2 * MLSTM_WIDTH), F32))
    return pl.pallas_call(
        functools.partial(_mlstm_body, n_sub=n_sub, seq_len=seq_len, has_state=has_state,
                          n_prev=len(prev), single_chunk=n_chunk == 1, par=par),
        grid=(n_tiles // par, n_chunk),
        in_specs=in_specs,
        out_specs=[act_spec(MLSTM_WIDTH)] + state_specs,
        out_shape=out_shape,
        scratch_shapes=scratch,
        input_output_aliases=aliases,
        compiler_params=_params(("parallel", "arbitrary")),
        name="mlstm",
    )(*args)


def _post_body(x_ref, y_ref, hm_ref, g_ref, wo_ref, wgs_ref, wgm_ref, wglu_ref, wsu_ref, wmu_ref,
               wout_ref, o_ref):
    x = x_ref[...]
    h = _rms(x, g_ref[...]).astype(BF16)
    ys = jax.nn.gelu(y_ref[...])
    ys = ys * jax.nn.sigmoid(_dot(ys.astype(BF16), wglu_ref[...]))
    hm = hm_ref[...] * jax.nn.sigmoid(_dot(h, wo_ref[...]))
    merged = (jax.nn.sigmoid(_dot(h, wgs_ref[...])) * _dot(ys.astype(BF16), wsu_ref[...])
              + jax.nn.sigmoid(_dot(h, wgm_ref[...])) * _dot(hm.astype(BF16), wmu_ref[...]))
    o_ref[...] = x + _dot(merged.astype(BF16), wout_ref[...])


def _post(x, y, hm, g, wo, wgs, wgm, wglu, wsu, wmu, wout, layer):
    rows = x.shape[0]
    tm = min(ROW_TILE, rows)
    rb = lambda i: (i, 0)
    sq = _layer_spec((D_MODEL, D_MODEL), layer)
    return pl.pallas_call(
        _post_body,
        grid=(rows // tm,),
        in_specs=[pl.BlockSpec((tm, D_MODEL), rb), pl.BlockSpec((tm, S5_WIDTH), rb),
                  pl.BlockSpec((tm, MLSTM_WIDTH), rb),
                  _layer_spec((1, D_MODEL), layer), sq, sq, sq,
                  _layer_spec((S5_WIDTH, S5_WIDTH), layer), _layer_spec((S5_WIDTH, D_MODEL), layer), sq, sq],
        out_specs=pl.BlockSpec((tm, D_MODEL), rb),
        out_shape=jax.ShapeDtypeStruct((rows, D_MODEL), F32),
        compiler_params=_params(("parallel",)),
        name="post",
    )(x, y, hm, g, wo, wgs, wgm, wglu, wsu, wmu, wout)


def _block_diag(t):
    depth, nb, gpb, a, c = t.shape
    eye = jnp.eye(gpb, dtype=t.dtype)
    return jnp.einsum("ljgac,gk->ljgakc", t, eye).reshape(depth, nb, gpb * a, gpb * c)


def kernel(x_prompt, x_sample, state_s5_re, state_s5_im, state_mlstm_c, state_mlstm_n, state_mlstm_m, state_conv, p_prompt, p_sample, g_ffn1, w1_gate, w1_up, w1_down, g_mix, w_in, s5_lambda_re, s5_lambda_im, s5_log_dt, s5_b_re, s5_b_im, s5_c_re, s5_c_im, s5_d, s5_w_glu, w_s5_up, conv_w, conv_b, b_igate, b_fgate, g_mhead, w_m_up, w_out, g_ffn2, w2_gate, w2_up, w2_down, g_ple, w_ple, w_ple_gate, g_final):
    depth = w_in.shape[0]
    bp, sp, _ = x_prompt.shape
    bs, ss, _ = x_sample.shape
    hh = MLSTM_HEADS
    gpb = S5_GROUPS // S5_BLOCKS

    bf = lambda w: w.astype(BF16)
    cuts = [0, S5_WIDTH, S5_WIDTH + 2 * MLSTM_WIDTH, S5_WIDTH + 3 * MLSTM_WIDTH, S5_WIDTH + 4 * MLSTM_WIDTH]
    c_gate = cuts[4] + 2 * MLSTM_HEADS
    w_u = bf(w_in[:, :, cuts[0]:cuts[1]])
    w_qk = bf(w_in[:, :, cuts[1]:cuts[2]])
    w_v = bf(w_in[:, :, cuts[2]:cuts[3]])
    w_o = bf(w_in[:, :, cuts[3]:cuts[4]])
    w_gc = bf(jnp.pad(w_in[:, :, cuts[4]:c_gate], ((0, 0), (0, 0), (0, LANES - 2 * MLSTM_HEADS))))
    w_gs = bf(w_in[:, :, c_gate:c_gate + D_MODEL])
    w_gm = bf(w_in[:, :, c_gate + D_MODEL:])
    w1g, w1u, w1d = bf(w1_gate), bf(w1_up), bf(w1_down)
    w2g, w2u, w2d = bf(w2_gate), bf(w2_up), bf(w2_down)
    wglu, wsu, wmu, wout = bf(s5_w_glu), bf(w_s5_up), bf(w_m_up), bf(w_out)
    wple, wpg = bf(w_ple), bf(w_ple_gate)
    row = lambda g: g.reshape(depth, 1, -1)
    gf1, gmx, gf2, gpl, gmh, s5d, cvb = (row(g_ffn1), row(g_mix), row(g_ffn2), row(g_ple),
                                          row(g_mhead), row(s5_d), row(conv_b))
    gfin = g_final.reshape(1, D_MODEL)
    bias_c = jnp.pad(jnp.concatenate([b_igate, b_fgate], axis=1),
                     ((0, 0), (0, LANES - 2 * MLSTM_HEADS))).reshape(depth, 1, LANES)

    lanes3 = lambda t: t.reshape(depth, 1, S5_LANES)
    ldt = jnp.broadcast_to(s5_log_dt[:, :, None], (depth, S5_GROUPS, S5_STATE))
    to_cols = lambda t: jnp.transpose(t, (0, 3, 1, 2)).reshape(depth, S5_GROUP, S5_LANES)
    a_re, a_im, bb_re, bb_im = _s5_params(lanes3(s5_lambda_re), lanes3(s5_lambda_im), lanes3(ldt),
                                          to_cols(s5_b_re), to_cols(s5_b_im))
    to_blk = lambda t: jnp.transpose(t.reshape(depth, S5_GROUP, S5_BLOCKS, gpb, S5_STATE), (0, 2, 3, 1, 4))
    bmat = bf(jnp.concatenate([_block_diag(to_blk(bb_re)), _block_diag(to_blk(bb_im))], axis=-1))
    c_blk = lambda t: jnp.transpose(t.reshape(depth, S5_BLOCKS, gpb, S5_GROUP, S5_STATE), (0, 1, 2, 4, 3))
    cmat_re = bf(_block_diag(c_blk(s5_c_re)))
    cmat_im = bf(_block_diag(c_blk(s5_c_im)))
    s5_time_p = S5_ROWS // bp
    perm_p = _time_major_perm(bp, s5_time_p)
    perm_s = _time_major_perm(bs, ss)

    xp = x_prompt.reshape(bp * sp, D_MODEL)
    xs = x_sample.reshape(bs * ss, D_MODEL)
    pp = p_prompt.reshape(depth, bp * sp, D_PLE)
    ps = p_sample.reshape(depth, bs * ss, D_PLE)
    to_blocks = lambda t: jnp.swapaxes(t.reshape(depth, -1, S5_BLOCKS, S5_BLOCK_ST), 1, 2)
    from_blocks = lambda t: jnp.swapaxes(t, 1, 2).reshape(depth, -1, S5_GROUPS, S5_STATE)
    a_re = a_re.reshape(depth, S5_BLOCKS, 1, S5_BLOCK_ST)
    a_im = a_im.reshape(depth, S5_BLOCKS, 1, S5_BLOCK_ST)
    s5_state = (to_blocks(state_s5_re), to_blocks(state_s5_im))
    lstm_state = (state_mlstm_c, state_mlstm_n, state_mlstm_m.reshape(depth, bs, 1, hh), state_conv)

    s5_out_p, s5_out_s, lstm_out_p, lstm_out_s = [], [], [], []
    for i in range(depth):
        final = i == depth - 1
        ffn1_w = (gf1, w1g, w1u, w1d)
        proj_w = (gmx, w_u, w_qk, w_v, w_gc)
        s5_w = (a_re, a_im, bmat, cmat_re, cmat_im, s5d)
        lstm_w = (conv_w, cvb, bias_c, gmh)
        mix_w = (gmx, w_o, w_gs, w_gm, wglu, wsu, wmu, wout)
        ple_w = (gf2, w2g, w2u, w2d, gpl, wple, wpg, gfin)

        xp = _ffn(xp, *ffn1_w, layer=i)
        u, qk, v, gc = _inproj(xp, *proj_w, layer=i, v_dtype=BF16)
        y, s5_re, s5_im = _s5(u, perm_p, perm_p.T, *s5_w, None, layer=i, n_batch=bp, n_time=s5_time_p)
        s5_out_p.append((s5_re, s5_im))
        hm, *lstm_out_p = _mlstm(qk, v, gc, *lstm_w, None, lstm_out_p, layer=i, depth=depth,
                                 n_tiles=bp, n_sub=1, seq_len=MLSTM_CHUNK, n_chunk=sp // MLSTM_CHUNK,
                                 par=PROMPT_TILES_PER_STEP)
        xp = _post(xp, y.reshape(bp * sp, S5_WIDTH), hm.reshape(bp * sp, MLSTM_WIDTH), *mix_w, layer=i)
        xp = _ffn_ple(xp, pp, *ple_w, layer=i, final=final)

        xs = _ffn(xs, *ffn1_w, layer=i)
        u, qk, v, gc = _inproj(xs, *proj_w, layer=i, v_dtype=F32)
        y, s5_re, s5_im = _s5(u, perm_s, perm_s.T, *s5_w, s5_state, layer=i, n_batch=bs, n_time=ss)
        s5_out_s.append((s5_re, s5_im))
        hm, *lstm_out_s = _mlstm(qk, v, gc, *lstm_w, lstm_state, lstm_out_s, layer=i, depth=depth,
                                 n_tiles=bs // SAMPLE_SEQS_PER_TILE, n_sub=SAMPLE_SEQS_PER_TILE,
                                 seq_len=ss, n_chunk=1, par=SAMPLE_TILES_PER_STEP)
        xs = _post(xs, y.reshape(bs * ss, S5_WIDTH), hm.reshape(bs * ss, MLSTM_WIDTH), *mix_w, layer=i)
        xs = _ffn_ple(xs, ps, *ple_w, layer=i, final=final)

    def states(n, s5_out, lstm_out):
        c_n, n_n, m_n, conv_n = lstm_out
        s5_re, s5_im = (jnp.stack(parts) for parts in zip(*s5_out))
        return (from_blocks(s5_re), from_blocks(s5_im), c_n, n_n, m_n.reshape(depth, n, hh), conv_n)

    return ((xp.reshape(bp, sp, D_MODEL), xs.reshape(bs, ss, D_MODEL))
            + states(bp, s5_out_p, lstm_out_p) + states(bs, s5_out_s, lstm_out_s))
```

```python
import functools
import itertools

import jax
import jax.numpy as jnp
from jax import lax
from jax.experimental import pallas as pl
from jax.experimental.pallas import tpu as pltpu

D_MODEL = 1024
D_PLE = 256
D_FF = 2048
S5_WIDTH = 512
S5_GROUP = 16
S5_GROUPS = S5_WIDTH // S5_GROUP
S5_STATE = 64
S5_LANES = S5_GROUPS * S5_STATE
MLSTM_WIDTH = 1024
MLSTM_HEADS = 4
MLSTM_HEAD_DIM = MLSTM_WIDTH // MLSTM_HEADS
MLSTM_CHUNK = 128
CONV_WIDTH = 4
CONV_HIST = CONV_WIDTH - 1
EPS = 1e-6

F32 = jnp.float32
BF16 = jnp.bfloat16
HIGHEST = lax.Precision.HIGHEST

SUBLANES = 8
LANES = 128
VMEM_LIMIT_BYTES = 56 * 1024 * 1024

ROW_TILE = 1024
S5_ROWS = 512
S5_BLOCKS = 4
S5_BLOCK_IN = S5_WIDTH // S5_BLOCKS
S5_BLOCK_ST = S5_LANES // S5_BLOCKS
SAMPLE_SEQS_PER_TILE = 2
PROMPT_TILES_PER_STEP = 4
SAMPLE_TILES_PER_STEP = 4


def _params(sem):
    return pltpu.CompilerParams(dimension_semantics=sem, vmem_limit_bytes=VMEM_LIMIT_BYTES)


def _layer_spec(shape, layer):
    nd = len(shape)
    return pl.BlockSpec((None,) + tuple(shape), lambda *_: (layer,) + (0,) * nd,
                        pipeline_mode=pl.Buffered(1))


def _const_spec(shape):
    nd = len(shape)
    return pl.BlockSpec(tuple(shape), lambda *_: (0,) * nd, pipeline_mode=pl.Buffered(1))


_ANY = pl.BlockSpec(memory_space=pl.ANY)


def _rms(x, g):
    return x * lax.rsqrt(jnp.mean(x * x, axis=-1, keepdims=True) + EPS) * g


def _dot(a, b):
    return jnp.dot(a, b, preferred_element_type=F32)


def _dot_nt(a, b, **kw):
    return lax.dot_general(a, b, (((1,), (1,)), ((), ())), preferred_element_type=F32, **kw)


def _dot_tn(a, b):
    return lax.dot_general(a, b, (((0,), (0,)), ((), ())), preferred_element_type=F32)


def _ffn_body(x_ref, g_ref, wg_ref, wu_ref, wd_ref, o_ref):
    x = x_ref[...]
    h = _rms(x, g_ref[...]).astype(BF16)
    z = (jax.nn.silu(_dot(h, wg_ref[...])) * _dot(h, wu_ref[...])).astype(BF16)
    o_ref[...] = x + 0.5 * _dot(z, wd_ref[...])


def _ffn_ple_body(x_ref, p_ref, g_ref, wg_ref, wu_ref, wd_ref, gp_ref, wp_ref, wpg_ref, gf_ref, o_ref,
                  *, final):
    x = x_ref[...]
    h = _rms(x, g_ref[...]).astype(BF16)
    z = (jax.nn.silu(_dot(h, wg_ref[...])) * _dot(h, wu_ref[...])).astype(BF16)
    x = x + 0.5 * _dot(z, wd_ref[...])
    gate = jax.nn.sigmoid(_dot(_rms(x, gp_ref[...]).astype(BF16), wpg_ref[...]))
    x = x + _dot(p_ref[...].astype(BF16), wp_ref[...]) * gate
    if final:
        x = _rms(x, gf_ref[...])
    o_ref[...] = x


def _ffn(x, g, wg, wu, wd, layer):
    rows = x.shape[0]
    tm = min(ROW_TILE, rows)
    row_spec = pl.BlockSpec((tm, D_MODEL), lambda i: (i, 0))
    return pl.pallas_call(
        _ffn_body,
        grid=(rows // tm,),
        in_specs=[row_spec, _layer_spec((1, D_MODEL), layer), _layer_spec((D_MODEL, D_FF), layer),
                  _layer_spec((D_MODEL, D_FF), layer), _layer_spec((D_FF, D_MODEL), layer)],
        out_specs=row_spec,
        out_shape=jax.ShapeDtypeStruct((rows, D_MODEL), F32),
        compiler_params=_params(("parallel",)),
        name="ffn",
    )(x, g, wg, wu, wd)


def _ffn_ple(x, p, g, wg, wu, wd, gp, wp, wpg, gf, layer, final):
    rows = x.shape[0]
    tm = min(ROW_TILE, rows)
    row_spec = pl.BlockSpec((tm, D_MODEL), lambda i: (i, 0))
    return pl.pallas_call(
        functools.partial(_ffn_ple_body, final=final),
        grid=(rows // tm,),
        in_specs=[row_spec, pl.BlockSpec((None, tm, D_PLE), lambda i: (layer, i, 0)),
                  _layer_spec((1, D_MODEL), layer), _layer_spec((D_MODEL, D_FF), layer),
                  _layer_spec((D_MODEL, D_FF), layer), _layer_spec((D_FF, D_MODEL), layer),
                  _layer_spec((1, D_MODEL), layer), _layer_spec((D_PLE, D_MODEL), layer),
                  _layer_spec((D_MODEL, D_MODEL), layer), _const_spec((1, D_MODEL))],
        out_specs=row_spec,
        out_shape=jax.ShapeDtypeStruct((rows, D_MODEL), F32),
        compiler_params=_params(("parallel",)),
        name="ffn_ple",
    )(x, p, g, wg, wu, wd, gp, wp, wpg, gf)


def _inproj_body(x_ref, g_ref, wu_ref, wqk_ref, wv_ref, wgc_ref, u_ref, qk_ref, v_ref, gc_ref):
    h = _rms(x_ref[...], g_ref[...]).astype(BF16)
    u_ref[...] = _dot(h, wu_ref[...])
    qk_ref[...] = _dot(h, wqk_ref[...])
    v_ref[...] = _dot(h, wv_ref[...]).astype(v_ref.dtype)
    gc_ref[...] = _dot(h, wgc_ref[...])


def _inproj(x, g, wu, wqk, wv, wgc, layer, v_dtype):
    rows = x.shape[0]
    tm = min(ROW_TILE, rows)
    rb = lambda i: (i, 0)
    widths = (S5_WIDTH, 2 * MLSTM_WIDTH, MLSTM_WIDTH, LANES)
    dtypes = (F32, F32, v_dtype, F32)
    return pl.pallas_call(
        _inproj_body,
        grid=(rows // tm,),
        in_specs=[pl.BlockSpec((tm, D_MODEL), rb), _layer_spec((1, D_MODEL), layer)]
        + [_layer_spec((D_MODEL, w), layer) for w in widths],
        out_specs=[pl.BlockSpec((tm, w), rb) for w in widths],
        out_shape=[jax.ShapeDtypeStruct((rows, w), dt) for w, dt in zip(widths, dtypes)],
        compiler_params=_params(("parallel",)),
        name="inproj",
    )(x, g, wu, wqk, wv, wgc)


def _s5_param_body(lre_ref, lim_ref, ldt_ref, bre_ref, bim_ref, are_ref, aim_ref, bbre_ref, bbim_ref):
    lre = lre_ref[0]
    lim = lim_ref[0]
    dt = jnp.exp(ldt_ref[0])
    mag = jnp.exp(lre * dt)
    a_re = mag * jnp.cos(lim * dt)
    a_im = mag * jnp.sin(lim * dt)
    den = lre * lre + lim * lim
    pr = a_re - 1.0
    w_re = (pr * lre + a_im * lim) / den
    w_im = (a_im * lre - pr * lim) / den
    are_ref[0] = a_re
    aim_ref[0] = a_im
    bbre_ref[0] = w_re * bre_ref[0] - w_im * bim_ref[0]
    bbim_ref[0] = w_re * bim_ref[0] + w_im * bre_ref[0]


def _s5_params(lre, lim, ldt, bre, bim):
    depth = lre.shape[0]
    vec = pl.BlockSpec((1, 1, S5_LANES), lambda i: (i, 0, 0))
    mat = pl.BlockSpec((1, S5_GROUP, S5_LANES), lambda i: (i, 0, 0))
    return pl.pallas_call(
        _s5_param_body,
        grid=(depth,),
        in_specs=[vec, vec, vec, mat, mat],
        out_specs=[vec, vec, mat, mat],
        out_shape=[jax.ShapeDtypeStruct((depth, 1, S5_LANES), F32)] * 2
        + [jax.ShapeDtypeStruct((depth, S5_GROUP, S5_LANES), F32)] * 2,
        compiler_params=_params(("parallel",)),
        name="s5_params",
    )(lre, lim, ldt, bre, bim)


def _s5_body(*refs, n_batch, n_time, has_state):
    weights = refs[:9]
    refs = refs[9:]
    h0 = None
    if has_state:
        h0 = refs[:2]
        refs = refs[2:]
    sr_ref, si_ref = refs[1:3]

    @pl.when(pl.program_id(0) == 0)
    def _():
        if h0 is not None:
            sr_ref[...] = h0[0][...]
            si_ref[...] = h0[1][...]
        else:
            sr_ref[...] = jnp.zeros_like(sr_ref)
            si_ref[...] = jnp.zeros_like(si_ref)

    _s5_chunk(*weights, *refs, n_batch=n_batch, n_time=n_time)


def _s5_chunk(u_ref, perm_ref, permt_ref, are_ref, aim_ref, bm_ref, cre_ref, cim_ref, d_ref,
              y_ref, sr_ref, si_ref, ub_s, bur_s, bui_s, hr_s, hi_s, y_s, *, n_batch, n_time):
    rows = n_batch * n_time
    n_bt = n_batch // SUBLANES

    u = u_ref[...].reshape(rows, S5_WIDTH)
    ub = _dot(perm_ref[...], u.astype(BF16)).astype(BF16)
    for j in range(S5_BLOCKS):
        ub_s[j] = ub[:, j * S5_BLOCK_IN:(j + 1) * S5_BLOCK_IN]

    def project_in(j):
        r = _dot(ub_s[j], bm_ref[j])
        bur_s[j] = r[:, :S5_BLOCK_ST]
        bui_s[j] = r[:, S5_BLOCK_ST:]

    def scan(j):
        a_re = jnp.broadcast_to(are_ref[j], (SUBLANES, S5_BLOCK_ST))
        a_im = jnp.broadcast_to(aim_ref[j], (SUBLANES, S5_BLOCK_ST))

        def advance(h, row):
            h_re, h_im = h
            tile = slice(row, row + SUBLANES)
            return (a_re * h_re - a_im * h_im + bur_s[j, tile, :],
                    a_re * h_im + a_im * h_re + bui_s[j, tile, :])

        def emit(row, first, second):
            pair = slice(row, row + 2 * SUBLANES)
            hr_s[j, pair, :] = jnp.concatenate([first[0], second[0]], axis=0).astype(BF16)
            hi_s[j, pair, :] = jnp.concatenate([first[1], second[1]], axis=0).astype(BF16)

        if n_bt == 1:
            h = (sr_ref[j], si_ref[j])
            for t in range(0, n_time, 2):
                h1 = advance(h, t * SUBLANES)
                h = advance(h1, (t + 1) * SUBLANES)
                emit(t * SUBLANES, h1, h)
            sr_ref[j], si_ref[j] = h
        else:
            for bt in range(0, n_bt, 2):
                rows_a = slice(bt * SUBLANES, (bt + 1) * SUBLANES)
                rows_b = slice((bt + 1) * SUBLANES, (bt + 2) * SUBLANES)
                ha = (sr_ref[j, rows_a, :], si_ref[j, rows_a, :])
                hb = (sr_ref[j, rows_b, :], si_ref[j, rows_b, :])
                for t in range(n_time):
                    row = t * n_batch + bt * SUBLANES
                    ha = advance(ha, row)
                    hb = advance(hb, row + SUBLANES)
                    emit(row, ha, hb)
                sr_ref[j, rows_a, :], si_ref[j, rows_a, :] = ha
                sr_ref[j, rows_b, :], si_ref[j, rows_b, :] = hb

    def project_out(j):
        y_s[j] = _dot(hr_s[j], cre_ref[j]) - _dot(hi_s[j], cim_ref[j])

    project_in(0)

    def pipelined(j, carry):
        project_in(j + 1)
        scan(j)
        project_out(j)
        return carry

    lax.fori_loop(0, S5_BLOCKS - 1, pipelined, 0)
    scan(S5_BLOCKS - 1)
    project_out(S5_BLOCKS - 1)
    y = jnp.concatenate([y_s[j] for j in range(S5_BLOCKS)], axis=1)
    y_hi = y.astype(BF16)
    y_lo = (y - y_hi.astype(F32)).astype(BF16)
    permt = permt_ref[...]
    y = _dot(permt, y_hi) + _dot(permt, y_lo) + d_ref[...] * u
    y_ref[...] = y.reshape(y_ref.shape)


def _s5(u, perm, permt, a_re, a_im, bm, cre, cim, d, h0, layer, n_batch, n_time):
    seq = u.shape[0] // n_batch
    rows = n_batch * n_time
    has_state = h0 is not None
    assert n_batch % (2 * SUBLANES) == 0 or (n_batch == SUBLANES and n_time % 2 == 0)
    lead, blk_rows = (n_batch, n_time) if seq > n_time else (1, rows)
    u = u.reshape(lead, u.shape[0] // lead, S5_WIDTH)
    u_spec = pl.BlockSpec((lead, blk_rows, S5_WIDTH), lambda c: (0, c, 0))
    state_in_spec = pl.BlockSpec((None, S5_BLOCKS, n_batch, S5_BLOCK_ST), lambda c: (layer, 0, 0, 0))
    state_spec = pl.BlockSpec((S5_BLOCKS, n_batch, S5_BLOCK_ST), lambda c: (0, 0, 0))
    state_shape = jax.ShapeDtypeStruct((S5_BLOCKS, n_batch, S5_BLOCK_ST), F32)
    in_specs = [u_spec,
                _const_spec((rows, rows)), _const_spec((rows, rows)),
                _layer_spec((S5_BLOCKS, 1, S5_BLOCK_ST), layer), _layer_spec((S5_BLOCKS, 1, S5_BLOCK_ST), layer),
                _layer_spec((S5_BLOCKS, S5_BLOCK_IN, 2 * S5_BLOCK_ST), layer),
                _layer_spec((S5_BLOCKS, S5_BLOCK_ST, S5_BLOCK_IN), layer),
                _layer_spec((S5_BLOCKS, S5_BLOCK_ST, S5_BLOCK_IN), layer),
                _layer_spec((1, S5_WIDTH), layer)]
    args = [u, perm, permt, a_re, a_im, bm, cre, cim, d]
    if has_state:
        in_specs += [state_in_spec, state_in_spec]
        args += list(h0)
    return pl.pallas_call(
        functools.partial(_s5_body, n_batch=n_batch, n_time=n_time, has_state=has_state),
        grid=(seq // n_time,),
        in_specs=in_specs,
        out_specs=[u_spec, state_spec, state_spec],
        out_shape=[jax.ShapeDtypeStruct(u.shape, F32), state_shape, state_shape],
        scratch_shapes=[pltpu.VMEM((S5_BLOCKS, rows, S5_BLOCK_IN), BF16),
                        pltpu.VMEM((S5_BLOCKS, rows, S5_BLOCK_ST), F32),
                        pltpu.VMEM((S5_BLOCKS, rows, S5_BLOCK_ST), F32),
                        pltpu.VMEM((S5_BLOCKS, rows, S5_BLOCK_ST), BF16),
                        pltpu.VMEM((S5_BLOCKS, rows, S5_BLOCK_ST), BF16),
                        pltpu.VMEM((S5_BLOCKS, rows, S5_BLOCK_IN), F32)],
        compiler_params=_params(("arbitrary",)),
        name="s5",
    )(*args)


def _time_major_perm(n_batch, n_time):
    r = jnp.arange(n_batch * n_time)
    src = (r % n_batch) * n_time + r // n_batch
    return (src[:, None] == r[None, :]).astype(BF16)


def _mlstm_tile(g, qk_ref, v_ref, gc_ref, cw_ref, cb_ref, bc_ref,
                c_in, n_in, m_in, hm_ref, c_ref, n_ref, m_ref, conv_ref, xp_s, qk_s,
                *, n_sub, seq_len, carry_hist):
    chunk = n_sub * seq_len
    region = SUBLANES + -(-seq_len // SUBLANES) * SUBLANES
    hist = SUBLANES
    slots = [g * n_sub + j for j in range(n_sub)]

    for j, slot in enumerate(slots):
        base = slot * region
        x_raw = qk_ref[g, j * seq_len:(j + 1) * seq_len, :]
        xp_s[base + hist:base + hist + seq_len, :] = x_raw
        if seq_len % SUBLANES == 0:
            x3 = x_raw.reshape(seq_len // SUBLANES, SUBLANES, 2 * MLSTM_WIDTH)
            prev = xp_s[base:base + hist, :]
            sub = lax.broadcasted_iota(jnp.int32, (1, SUBLANES, 1), 1)
            acc = cb_ref[...] + x3 * cw_ref[CONV_HIST:CONV_WIDTH, :]
            for k in range(1, CONV_WIDTH):
                rot = pltpu.roll(x3, k, axis=1)
                before = jnp.concatenate([pltpu.roll(prev, k, axis=0)[None], rot[:-1]], axis=0)
                acc = acc + jnp.where(sub < k, before, rot) * cw_ref[CONV_HIST - k:CONV_WIDTH - k, :]
            acc = acc.reshape(seq_len, 2 * MLSTM_WIDTH)
        else:
            acc = jnp.broadcast_to(cb_ref[...], (seq_len, 2 * MLSTM_WIDTH))
            for i in range(CONV_WIDTH):
                off = base + hist - CONV_HIST + i
                acc = acc + xp_s[off:off + seq_len, :] * cw_ref[i:i + 1, :]
        new_hist = xp_s[base + hist + seq_len - CONV_HIST:base + hist + seq_len, :]
        conv_ref[slot] = new_hist
        if carry_hist:
            xp_s[base + hist - CONV_HIST:base + hist, :] = new_hist
        if n_sub > 1:
            qk_s[g, j * seq_len:(j + 1) * seq_len, :] = acc
        yield
    qk = jax.nn.silu(qk_s[g] if n_sub > 1 else acc)
    q_all = qk[:, :MLSTM_WIDTH] * (MLSTM_HEAD_DIM ** -0.5)
    k_all = qk[:, MLSTM_WIDTH:]
    v_all = v_ref[g]
    yield

    def seq_of(idx):
        s = jnp.zeros(idx.shape, F32)
        for j in range(1, n_sub):
            s = s + jnp.where(idx >= j * seq_len, 1.0, 0.0)
        return s

    row_seq = seq_of(lax.broadcasted_iota(jnp.int32, (chunk, 1), 0))
    col_seq = seq_of(lax.broadcasted_iota(jnp.int32, (1, chunk), 1))
    t_id = lax.broadcasted_iota(jnp.int32, (chunk, chunk), 0)
    s_id = lax.broadcasted_iota(jnp.int32, (chunk, chunk), 1)
    same = jnp.where(row_seq == col_seq, 1.0, 0.0) if n_sub > 1 else jnp.ones((chunk, chunk), F32)
    tril = jnp.where(s_id <= t_id, same, 0.0)
    triu = jnp.where(t_id <= s_id, same, 0.0)
    causal = tril > 0.5
    pick = (lax.broadcasted_iota(jnp.int32, (SUBLANES, LANES), 0)
            == lax.broadcasted_iota(jnp.int32, (SUBLANES, LANES), 1)).astype(F32)

    li_col = gc_ref[g] + bc_ref[...]
    li_row = _dot_nt(pick, li_col, precision=HIGHEST)
    lf_col = jax.nn.log_sigmoid(li_col)
    yield
    lf_row = jax.nn.log_sigmoid(li_row)
    b_col = jnp.dot(tril, lf_col, preferred_element_type=F32, precision=HIGHEST)
    b_row = jnp.dot(lf_row, triu, preferred_element_type=F32, precision=HIGHEST)
    yield

    def per_row(vals):
        out = vals[0]
        for j in range(1, n_sub):
            out = jnp.where(row_seq == j, vals[j], out)
        return out

    m_new_parts = [[] for _ in range(n_sub)]
    for h in range(MLSTM_HEADS):
        hs = slice(h * MLSTM_HEAD_DIM, (h + 1) * MLSTM_HEAD_DIM)
        bc = b_col[:, MLSTM_HEADS + h:MLSTM_HEADS + h + 1]
        br = b_row[MLSTM_HEADS + h:MLSTM_HEADS + h + 1, :]
        lic = li_col[:, h:h + 1]
        lir = li_row[h:h + 1, :]
        m_prev = [m_in[s][:, h:h + 1] for s in slots]
        c_old = [c_in[s, h] for s in slots]
        n_old = [n_in[s, h:h + 1, :] for s in slots]
        qf = q_all[:, hs]
        kf = k_all[:, hs]
        qb = qf.astype(BF16)
        vb = v_all[:, hs].astype(BF16)

        dmat = jnp.where(causal, bc - br + lir, -jnp.inf)
        inter = bc + per_row(m_prev)
        m_t = jnp.maximum(inter, jnp.max(dmat, axis=-1, keepdims=True))
        s_qk = _dot_nt(qb, kf.astype(BF16))
        q_c = per_row([_dot(qb, c.astype(BF16)) for c in c_old])
        yield
        w = jnp.exp(dmat - m_t)
        scores = s_qk * w
        a = jnp.exp(inter - m_t)
        num = _dot(scores.astype(BF16), vb) + a * q_c
        nq = (jnp.sum(scores, axis=-1, keepdims=True)
              + a * jnp.sum(qf * per_row(n_old), axis=-1, keepdims=True))
        yield
        hout = num / jnp.maximum(jnp.abs(nq), jnp.exp(-m_t))
        hm_ref[g, :, hs] = hout
        yield

        for j, slot in enumerate(slots):
            last = (j + 1) * seq_len - 1
            b_last = br[:, last:last + 1]
            g_row = b_last - br + lir
            if n_sub > 1:
                g_row = jnp.where(col_seq == j, g_row, -jnp.inf)
            m_new = jnp.maximum(b_last + m_prev[j], jnp.max(g_row, axis=-1, keepdims=True))
            wk = jnp.exp(b_last - bc + lic - m_new)
            if n_sub > 1:
                wk = jnp.where(row_seq == j, wk, 0.0)
            decay = jnp.exp(b_last + m_prev[j] - m_new)
            kw = kf * wk
            c_ref[slot, h] = decay * c_old[j] + _dot_tn(kw.astype(BF16), vb)
            n_ref[slot, h:h + 1, :] = decay * n_old[j] + jnp.sum(kw, axis=0, keepdims=True)
            m_new_parts[j].append(m_new)
            yield
    for j, slot in enumerate(slots):
        m_ref[slot] = jnp.concatenate(m_new_parts[j], axis=1)


def _mlstm_body(*refs, n_sub, seq_len, has_state, n_prev, single_chunk, par):
    qk_ref, v_ref, gc_ref, cw_ref, cb_ref, bc_ref = refs[:6]
    refs = refs[6:]
    c0_ref = n0_ref = m0_ref = conv0_ref = None
    if has_state:
        c0_ref, n0_ref, m0_ref, conv0_ref = refs[:4]
        refs = refs[4:]
    hm_ref, c_ref, n_ref, m_ref, conv_ref = refs[n_prev:n_prev + 5]
    scratch = refs[n_prev + 5:]
    xp_s = scratch[0]
    qk_s = scratch[1] if n_sub > 1 else None

    from_input = has_state and single_chunk
    c_in, n_in, m_in = (c0_ref, n0_ref, m0_ref) if from_input else (c_ref, n_ref, m_ref)

    @pl.when(pl.program_id(1) == 0)
    def _():
        _mlstm_init(c_ref, n_ref, m_ref, xp_s, None if from_input else (c0_ref, n0_ref, m0_ref), conv0_ref,
                    state_from_input=from_input)

    tiles = [_mlstm_tile(g, qk_ref, v_ref, gc_ref, cw_ref, cb_ref, bc_ref,
                         c_in, n_in, m_in, hm_ref, c_ref, n_ref, m_ref, conv_ref, xp_s, qk_s,
                         n_sub=n_sub, seq_len=seq_len, carry_hist=not single_chunk) for g in range(par)]
    order = itertools.zip_longest(*tiles) if single_chunk else itertools.chain(*tiles)
    for _ in order:
        pass


def _mlstm_init(c_ref, n_ref, m_ref, xp_s, state0, conv0_ref, *, state_from_input):
    n_slots = c_ref.shape[0]
    region = xp_s.shape[0] // n_slots
    for slot in range(n_slots):
        xp_s[slot * region:slot * region + SUBLANES, :] = jnp.zeros((SUBLANES, 2 * MLSTM_WIDTH), F32)
        if conv0_ref is not None:
            xp_s[slot * region + SUBLANES - CONV_HIST:slot * region + SUBLANES, :] = conv0_ref[slot]
    if state_from_input:
        return
    for ref, ref0 in zip((c_ref, n_ref, m_ref), state0):
        ref[...] = jnp.zeros_like(ref) if ref0 is None else ref0[...]


def _mlstm(qk, v, gc, cw, cb, bias_c, state, prev, layer, depth, n_tiles, n_sub, seq_len, n_chunk, par):
    assert n_sub == 1 or n_chunk == 1
    assert n_tiles % par == 0
    hh, dh = MLSTM_HEADS, MLSTM_HEAD_DIM
    chunk = n_sub * seq_len
    n_slots = par * n_sub
    n_all = n_tiles * n_sub
    has_state = state is not None
    tile3 = lambda t: t.reshape(n_tiles, n_chunk * chunk, t.shape[-1])
    act_spec = lambda w: pl.BlockSpec((par, chunk, w), lambda b, c: (b, c, 0))
    st5 = lambda b, c: (layer, b, 0, 0, 0)
    st4 = lambda b, c: (layer, b, 0, 0)
    state_specs = [pl.BlockSpec((None, n_slots, hh, dh, dh), st5),
                   pl.BlockSpec((None, n_slots, hh, dh), st4),
                   pl.BlockSpec((None, n_slots, 1, hh), st4),
                   pl.BlockSpec((None, n_slots, CONV_HIST, 2 * MLSTM_WIDTH), st4)]
    in_specs = [act_spec(2 * MLSTM_WIDTH), act_spec(MLSTM_WIDTH), act_spec(LANES),
                _layer_spec((CONV_WIDTH, 2 * MLSTM_WIDTH), layer), _layer_spec((1, 2 * MLSTM_WIDTH), layer),
                _layer_spec((1, LANES), layer)]
    args = [tile3(qk), tile3(v), tile3(gc), cw, cb, bias_c]
    if has_state:
        in_specs += state_specs
        args += list(state)
    aliases = {len(args) + k: 1 + k for k in range(len(prev))}
    in_specs += [_ANY] * len(prev)
    args += list(prev)
    out_shape = [jax.ShapeDtypeStruct((n_tiles, n_chunk * chunk, MLSTM_WIDTH), F32),
                 jax.ShapeDtypeStruct((depth, n_all, hh, dh, dh), F32),
                 jax.ShapeDtypeStruct((depth, n_all, hh, dh), F32),
                 jax.ShapeDtypeStruct((depth, n_all, 1, hh), F32),
                 jax.ShapeDtypeStruct((depth, n_all, CONV_HIST, 2 * MLSTM_WIDTH), F32)]
    region = SUBLANES + -(-seq_len // SUBLANES) * SUBLANES
    scratch = [pltpu.VMEM((n_slots * region, 2 * MLSTM_WIDTH), F32)]
    if n_sub > 1:
        scratch.append(pltpu.VMEM((par, chunk, 2 * MLSTM_WIDTH), F32))
    return pl.pallas_call(
        functools.partial(_mlstm_body, n_sub=n_sub, seq_len=seq_len, has_state=has_state,
                          n_prev=len(prev), single_chunk=n_chunk == 1, par=par),
        grid=(n_tiles // par, n_chunk),
        in_specs=in_specs,
        out_specs=[act_spec(MLSTM_WIDTH)] + state_specs,
        out_shape=out_shape,
        scratch_shapes=scratch,
        input_output_aliases=aliases,
        compiler_params=_params(("parallel", "arbitrary")),
        name="mlstm",
    )(*args)


def _post_body(x_ref, y_ref, hm_ref, g_ref, gm_ref, wo_ref, wgs_ref, wgm_ref, wglu_ref, wsu_ref, wmu_ref,
               wout_ref, o_ref):
    x = x_ref[...]
    h = _rms(x, g_ref[...]).astype(BF16)
    ys = jax.nn.gelu(y_ref[...])
    ys = ys * jax.nn.sigmoid(_dot(ys.astype(BF16), wglu_ref[...]))
    heads = []
    for k in range(MLSTM_HEADS):
        hk = hm_ref[:, k * MLSTM_HEAD_DIM:(k + 1) * MLSTM_HEAD_DIM]
        heads.append(hk * lax.rsqrt(jnp.mean(hk * hk, axis=-1, keepdims=True) + EPS))
    hm = jnp.concatenate(heads, axis=1) * gm_ref[...] * jax.nn.sigmoid(_dot(h, wo_ref[...]))
    merged = (jax.nn.sigmoid(_dot(h, wgs_ref[...])) * _dot(ys.astype(BF16), wsu_ref[...])
              + jax.nn.sigmoid(_dot(h, wgm_ref[...])) * _dot(hm.astype(BF16), wmu_ref[...]))
    o_ref[...] = x + _dot(merged.astype(BF16), wout_ref[...])


def _post(x, y, hm, g, gm, wo, wgs, wgm, wglu, wsu, wmu, wout, layer):
    rows = x.shape[0]
    tm = min(ROW_TILE, rows)
    rb = lambda i: (i, 0)
    sq = _layer_spec((D_MODEL, D_MODEL), layer)
    return pl.pallas_call(
        _post_body,
        grid=(rows // tm,),
        in_specs=[pl.BlockSpec((tm, D_MODEL), rb), pl.BlockSpec((tm, S5_WIDTH), rb),
                  pl.BlockSpec((tm, MLSTM_WIDTH), rb),
                  _layer_spec((1, D_MODEL), layer), _layer_spec((1, MLSTM_WIDTH), layer), sq, sq, sq,
                  _layer_spec((S5_WIDTH, S5_WIDTH), layer), _layer_spec((S5_WIDTH, D_MODEL), layer), sq, sq],
        out_specs=pl.BlockSpec((tm, D_MODEL), rb),
        out_shape=jax.ShapeDtypeStruct((rows, D_MODEL), F32),
        compiler_params=_params(("parallel",)),
        name="post",
    )(x, y, hm, g, gm, wo, wgs, wgm, wglu, wsu, wmu, wout)


def _block_diag(t):
    depth, nb, gpb, a, c = t.shape
    eye = jnp.eye(gpb, dtype=t.dtype)
    return jnp.einsum("ljgac,gk->ljgakc", t, eye).reshape(depth, nb, gpb * a, gpb * c)


def kernel(x_prompt, x_sample, state_s5_re, state_s5_im, state_mlstm_c, state_mlstm_n, state_mlstm_m, state_conv, p_prompt, p_sample, g_ffn1, w1_gate, w1_up, w1_down, g_mix, w_in, s5_lambda_re, s5_lambda_im, s5_log_dt, s5_b_re, s5_b_im, s5_c_re, s5_c_im, s5_d, s5_w_glu, w_s5_up, conv_w, conv_b, b_igate, b_fgate, g_mhead, w_m_up, w_out, g_ffn2, w2_gate, w2_up, w2_down, g_ple, w_ple, w_ple_gate, g_final):
    depth = w_in.shape[0]
    bp, sp, _ = x_prompt.shape
    bs, ss, _ = x_sample.shape
    hh = MLSTM_HEADS
    gpb = S5_GROUPS // S5_BLOCKS

    bf = lambda w: w.astype(BF16)
    cuts = [0, S5_WIDTH, S5_WIDTH + 2 * MLSTM_WIDTH, S5_WIDTH + 3 * MLSTM_WIDTH, S5_WIDTH + 4 * MLSTM_WIDTH]
    c_gate = cuts[4] + 2 * MLSTM_HEADS
    w_u = bf(w_in[:, :, cuts[0]:cuts[1]])
    w_qk = bf(w_in[:, :, cuts[1]:cuts[2]])
    w_v = bf(w_in[:, :, cuts[2]:cuts[3]])
    w_o = bf(w_in[:, :, cuts[3]:cuts[4]])
    w_gc = bf(jnp.pad(w_in[:, :, cuts[4]:c_gate], ((0, 0), (0, 0), (0, LANES - 2 * MLSTM_HEADS))))
    w_gs = bf(w_in[:, :, c_gate:c_gate + D_MODEL])
    w_gm = bf(w_in[:, :, c_gate + D_MODEL:])
    w1g, w1u, w1d = bf(w1_gate), bf(w1_up), bf(w1_down)
    w2g, w2u, w2d = bf(w2_gate), bf(w2_up), bf(w2_down)
    wglu, wsu, wmu, wout = bf(s5_w_glu), bf(w_s5_up), bf(w_m_up), bf(w_out)
    wple, wpg = bf(w_ple), bf(w_ple_gate)
    row = lambda g: g.reshape(depth, 1, -1)
    gf1, gmx, gf2, gpl, gmh, s5d, cvb = (row(g_ffn1), row(g_mix), row(g_ffn2), row(g_ple),
                                          row(g_mhead), row(s5_d), row(conv_b))
    gfin = g_final.reshape(1, D_MODEL)
    bias_c = jnp.pad(jnp.concatenate([b_igate, b_fgate], axis=1),
                     ((0, 0), (0, LANES - 2 * MLSTM_HEADS))).reshape(depth, 1, LANES)

    lanes3 = lambda t: t.reshape(depth, 1, S5_LANES)
    ldt = jnp.broadcast_to(s5_log_dt[:, :, None], (depth, S5_GROUPS, S5_STATE))
    to_cols = lambda t: jnp.transpose(t, (0, 3, 1, 2)).reshape(depth, S5_GROUP, S5_LANES)
    a_re, a_im, bb_re, bb_im = _s5_params(lanes3(s5_lambda_re), lanes3(s5_lambda_im), lanes3(ldt),
                                          to_cols(s5_b_re), to_cols(s5_b_im))
    to_blk = lambda t: jnp.transpose(t.reshape(depth, S5_GROUP, S5_BLOCKS, gpb, S5_STATE), (0, 2, 3, 1, 4))
    bmat = bf(jnp.concatenate([_block_diag(to_blk(bb_re)), _block_diag(to_blk(bb_im))], axis=-1))
    c_blk = lambda t: jnp.transpose(t.reshape(depth, S5_BLOCKS, gpb, S5_GROUP, S5_STATE), (0, 1, 2, 4, 3))
    cmat_re = bf(_block_diag(c_blk(s5_c_re)))
    cmat_im = bf(_block_diag(c_blk(s5_c_im)))
    s5_time_p = S5_ROWS // bp
    perm_p = _time_major_perm(bp, s5_time_p)
    perm_s = _time_major_perm(bs, ss)

    xp = x_prompt.reshape(bp * sp, D_MODEL)
    xs = x_sample.reshape(bs * ss, D_MODEL)
    pp = p_prompt.reshape(depth, bp * sp, D_PLE)
    ps = p_sample.reshape(depth, bs * ss, D_PLE)
    to_blocks = lambda t: jnp.swapaxes(t.reshape(depth, -1, S5_BLOCKS, S5_BLOCK_ST), 1, 2)
    from_blocks = lambda t: jnp.swapaxes(t, 1, 2).reshape(depth, -1, S5_GROUPS, S5_STATE)
    a_re = a_re.reshape(depth, S5_BLOCKS, 1, S5_BLOCK_ST)
    a_im = a_im.reshape(depth, S5_BLOCKS, 1, S5_BLOCK_ST)
    s5_state = (to_blocks(state_s5_re), to_blocks(state_s5_im))
    lstm_state = (state_mlstm_c, state_mlstm_n, state_mlstm_m.reshape(depth, bs, 1, hh), state_conv)

    s5_out_p, s5_out_s, lstm_out_p, lstm_out_s = [], [], [], []
    for i in range(depth):
        final = i == depth - 1
        ffn1_w = (gf1, w1g, w1u, w1d)
        proj_w = (gmx, w_u, w_qk, w_v, w_gc)
        s5_w = (a_re, a_im, bmat, cmat_re, cmat_im, s5d)
        lstm_w = (conv_w, cvb, bias_c)
        mix_w = (gmx, gmh, w_o, w_gs, w_gm, wglu, wsu, wmu, wout)
        ple_w = (gf2, w2g, w2u, w2d, gpl, wple, wpg, gfin)

        xp = _ffn(xp, *ffn1_w, layer=i)
        u, qk, v, gc = _inproj(xp, *proj_w, layer=i, v_dtype=BF16)
        y, s5_re, s5_im = _s5(u, perm_p, perm_p.T, *s5_w, None, layer=i, n_batch=bp, n_time=s5_time_p)
        s5_out_p.append((s5_re, s5_im))
        hm, *lstm_out_p = _mlstm(qk, v, gc, *lstm_w, None, lstm_out_p, layer=i, depth=depth,
                                 n_tiles=bp, n_sub=1, seq_len=MLSTM_CHUNK, n_chunk=sp // MLSTM_CHUNK,
                                 par=PROMPT_TILES_PER_STEP)
        xp = _post(xp, y.reshape(bp * sp, S5_WIDTH), hm.reshape(bp * sp, MLSTM_WIDTH), *mix_w, layer=i)
        xp = _ffn_ple(xp, pp, *ple_w, layer=i, final=final)

        xs = _ffn(xs, *ffn1_w, layer=i)
        u, qk, v, gc = _inproj(xs, *proj_w, layer=i, v_dtype=F32)
        y, s5_re, s5_im = _s5(u, perm_s, perm_s.T, *s5_w, s5_state, layer=i, n_batch=bs, n_time=ss)
        s5_out_s.append((s5_re, s5_im))
        hm, *lstm_out_s = _mlstm(qk, v, gc, *lstm_w, lstm_state, lstm_out_s, layer=i, depth=depth,
                                 n_tiles=bs // SAMPLE_SEQS_PER_TILE, n_sub=SAMPLE_SEQS_PER_TILE,
                                 seq_len=ss, n_chunk=1, par=SAMPLE_TILES_PER_STEP)
        xs = _post(xs, y.reshape(bs * ss, S5_WIDTH), hm.reshape(bs * ss, MLSTM_WIDTH), *mix_w, layer=i)
        xs = _ffn_ple(xs, ps, *ple_w, layer=i, final=final)

    def states(n, s5_out, lstm_out):
        c_n, n_n, m_n, conv_n = lstm_out
        s5_re, s5_im = (jnp.stack(parts) for parts in zip(*s5_out))
        return (from_blocks(s5_re), from_blocks(s5_im), c_n, n_n, m_n.reshape(depth, n, hh), conv_n)

    return ((xp.reshape(bp, sp, D_MODEL), xs.reshape(bs, ss, D_MODEL))
            + states(bp, s5_out_p, lstm_out_p) + states(bs, s5_out_s, lstm_out_s))
```

```python
import functools
import itertools

import jax
import jax.numpy as jnp
from jax import lax
from jax.experimental import pallas as pl
from jax.experimental.pallas import tpu as pltpu

D_MODEL = 1024
D_PLE = 256
D_FF = 2048
S5_WIDTH = 512
S5_GROUP = 16
S5_GROUPS = S5_WIDTH // S5_GROUP
S5_STATE = 64
S5_LANES = S5_GROUPS * S5_STATE
MLSTM_WIDTH = 1024
MLSTM_HEADS = 4
MLSTM_HEAD_DIM = MLSTM_WIDTH // MLSTM_HEADS
MLSTM_CHUNK = 128
CONV_WIDTH = 4
CONV_HIST = CONV_WIDTH - 1
EPS = 1e-6

F32 = jnp.float32
BF16 = jnp.bfloat16
HIGHEST = lax.Precision.HIGHEST

SUBLANES = 8
LANES = 128
VMEM_LIMIT_BYTES = 56 * 1024 * 1024

ROW_TILE = 1024
FUSED_ROW_TILE = 256
S5_ROWS = 512
S5_BLOCKS = 4
S5_BLOCK_IN = S5_WIDTH // S5_BLOCKS
S5_BLOCK_ST = S5_LANES // S5_BLOCKS
SAMPLE_SEQS_PER_TILE = 2
PROMPT_TILES_PER_STEP = 4
SAMPLE_TILES_PER_STEP = 4


def _params(sem):
    return pltpu.CompilerParams(dimension_semantics=sem, vmem_limit_bytes=VMEM_LIMIT_BYTES)


def _layer_spec(shape, layer):
    nd = len(shape)
    return pl.BlockSpec((None,) + tuple(shape), lambda *_: (layer,) + (0,) * nd,
                        pipeline_mode=pl.Buffered(1))


def _const_spec(shape):
    nd = len(shape)
    return pl.BlockSpec(tuple(shape), lambda *_: (0,) * nd, pipeline_mode=pl.Buffered(1))


_ANY = pl.BlockSpec(memory_space=pl.ANY)


def _rms(x, g):
    return x * lax.rsqrt(jnp.mean(x * x, axis=-1, keepdims=True) + EPS) * g


def _dot(a, b):
    return jnp.dot(a, b, preferred_element_type=F32)


def _dot_nt(a, b, **kw):
    return lax.dot_general(a, b, (((1,), (1,)), ((), ())), preferred_element_type=F32, **kw)


def _dot_tn(a, b):
    return lax.dot_general(a, b, (((0,), (0,)), ((), ())), preferred_element_type=F32)


def _ffn_body(x_ref, g_ref, wg_ref, wu_ref, wd_ref, o_ref):
    x = x_ref[...]
    h = _rms(x, g_ref[...]).astype(BF16)
    z = (jax.nn.silu(_dot(h, wg_ref[...])) * _dot(h, wu_ref[...])).astype(BF16)
    o_ref[...] = x + 0.5 * _dot(z, wd_ref[...])


def _ffn_ple_body(x_ref, p_ref, g_ref, wg_ref, wu_ref, wd_ref, gp_ref, wp_ref, wpg_ref, gf_ref, o_ref,
                  *, final):
    x = x_ref[...]
    h = _rms(x, g_ref[...]).astype(BF16)
    z = (jax.nn.silu(_dot(h, wg_ref[...])) * _dot(h, wu_ref[...])).astype(BF16)
    x = x + 0.5 * _dot(z, wd_ref[...])
    gate = jax.nn.sigmoid(_dot(_rms(x, gp_ref[...]).astype(BF16), wpg_ref[...]))
    x = x + _dot(p_ref[...].astype(BF16), wp_ref[...]) * gate
    if final:
        x = _rms(x, gf_ref[...])
    o_ref[...] = x


def _ffn(x, g, wg, wu, wd, layer):
    rows = x.shape[0]
    tm = min(ROW_TILE, rows)
    row_spec = pl.BlockSpec((tm, D_MODEL), lambda i: (i, 0))
    return pl.pallas_call(
        _ffn_body,
        grid=(rows // tm,),
        in_specs=[row_spec, _layer_spec((1, D_MODEL), layer), _layer_spec((D_MODEL, D_FF), layer),
                  _layer_spec((D_MODEL, D_FF), layer), _layer_spec((D_FF, D_MODEL), layer)],
        out_specs=row_spec,
        out_shape=jax.ShapeDtypeStruct((rows, D_MODEL), F32),
        compiler_params=_params(("parallel",)),
        name="ffn",
    )(x, g, wg, wu, wd)


def _ffn_ple(x, p, g, wg, wu, wd, gp, wp, wpg, gf, layer, final):
    rows = x.shape[0]
    tm = min(ROW_TILE, rows)
    row_spec = pl.BlockSpec((tm, D_MODEL), lambda i: (i, 0))
    return pl.pallas_call(
        functools.partial(_ffn_ple_body, final=final),
        grid=(rows // tm,),
        in_specs=[row_spec, pl.BlockSpec((None, tm, D_PLE), lambda i: (layer, i, 0)),
                  _layer_spec((1, D_MODEL), layer), _layer_spec((D_MODEL, D_FF), layer),
                  _layer_spec((D_MODEL, D_FF), layer), _layer_spec((D_FF, D_MODEL), layer),
                  _layer_spec((1, D_MODEL), layer), _layer_spec((D_PLE, D_MODEL), layer),
                  _layer_spec((D_MODEL, D_MODEL), layer), _const_spec((1, D_MODEL))],
        out_specs=row_spec,
        out_shape=jax.ShapeDtypeStruct((rows, D_MODEL), F32),
        compiler_params=_params(("parallel",)),
        name="ffn_ple",
    )(x, p, g, wg, wu, wd, gp, wp, wpg, gf)


def _inproj_body(x_ref, g_ref, wu_ref, wqk_ref, wv_ref, wgc_ref, u_ref, qk_ref, v_ref, gc_ref):
    h = _rms(x_ref[...], g_ref[...]).astype(BF16)
    u_ref[...] = _dot(h, wu_ref[...])
    qk_ref[...] = _dot(h, wqk_ref[...])
    v_ref[...] = _dot(h, wv_ref[...]).astype(v_ref.dtype)
    gc_ref[...] = _dot(h, wgc_ref[...])


def _inproj(x, g, wu, wqk, wv, wgc, layer, v_dtype):
    rows = x.shape[0]
    tm = min(ROW_TILE, rows)
    rb = lambda i: (i, 0)
    widths = (S5_WIDTH, 2 * MLSTM_WIDTH, MLSTM_WIDTH, LANES)
    dtypes = (F32, F32, v_dtype, F32)
    return pl.pallas_call(
        _inproj_body,
        grid=(rows // tm,),
        in_specs=[pl.BlockSpec((tm, D_MODEL), rb), _layer_spec((1, D_MODEL), layer)]
        + [_layer_spec((D_MODEL, w), layer) for w in widths],
        out_specs=[pl.BlockSpec((tm, w), rb) for w in widths],
        out_shape=[jax.ShapeDtypeStruct((rows, w), dt) for w, dt in zip(widths, dtypes)],
        compiler_params=_params(("parallel",)),
        name="inproj",
    )(x, g, wu, wqk, wv, wgc)


def _ffn_inproj_body(x_ref, g1_ref, wg_ref, wu_ref, wd_ref, g2_ref, wpu_ref, wqk_ref, wv_ref, wgc_ref,
                     x1_ref, u_ref, qk_ref, v_ref, gc_ref):
    _ffn_body(x_ref, g1_ref, wg_ref, wu_ref, wd_ref, x1_ref)
    _inproj_body(x1_ref, g2_ref, wpu_ref, wqk_ref, wv_ref, wgc_ref, u_ref, qk_ref, v_ref, gc_ref)


def _ffn_inproj(x, g1, wg, wu, wd, g2, wpu, wqk, wv, wgc, layer, v_dtype):
    rows = x.shape[0]
    tm = min(FUSED_ROW_TILE, rows)
    rb = lambda i: (i, 0)
    widths = (D_MODEL, S5_WIDTH, 2 * MLSTM_WIDTH, MLSTM_WIDTH, LANES)
    dtypes = (F32, F32, F32, v_dtype, F32)
    return pl.pallas_call(
        _ffn_inproj_body,
        grid=(rows // tm,),
        in_specs=[pl.BlockSpec((tm, D_MODEL), rb), _layer_spec((1, D_MODEL), layer),
                  _layer_spec((D_MODEL, D_FF), layer), _layer_spec((D_MODEL, D_FF), layer),
                  _layer_spec((D_FF, D_MODEL), layer), _layer_spec((1, D_MODEL), layer)]
        + [_layer_spec((D_MODEL, w), layer) for w in widths[1:]],
        out_specs=[pl.BlockSpec((tm, w), rb) for w in widths],
        out_shape=[jax.ShapeDtypeStruct((rows, w), dt) for w, dt in zip(widths, dtypes)],
        compiler_params=_params(("parallel",)),
        name="ffn_inproj",
    )(x, g1, wg, wu, wd, g2, wpu, wqk, wv, wgc)


def _s5_param_body(lre_ref, lim_ref, ldt_ref, bre_ref, bim_ref, are_ref, aim_ref, bbre_ref, bbim_ref):
    lre = lre_ref[0]
    lim = lim_ref[0]
    dt = jnp.exp(ldt_ref[0])
    mag = jnp.exp(lre * dt)
    a_re = mag * jnp.cos(lim * dt)
    a_im = mag * jnp.sin(lim * dt)
    den = lre * lre + lim * lim
    pr = a_re - 1.0
    w_re = (pr * lre + a_im * lim) / den
    w_im = (a_im * lre - pr * lim) / den
    are_ref[0] = a_re
    aim_ref[0] = a_im
    bbre_ref[0] = w_re * bre_ref[0] - w_im * bim_ref[0]
    bbim_ref[0] = w_re * bim_ref[0] + w_im * bre_ref[0]


def _s5_params(lre, lim, ldt, bre, bim):
    depth = lre.shape[0]
    vec = pl.BlockSpec((1, 1, S5_LANES), lambda i: (i, 0, 0))
    mat = pl.BlockSpec((1, S5_GROUP, S5_LANES), lambda i: (i, 0, 0))
    return pl.pallas_call(
        _s5_param_body,
        grid=(depth,),
        in_specs=[vec, vec, vec, mat, mat],
        out_specs=[vec, vec, mat, mat],
        out_shape=[jax.ShapeDtypeStruct((depth, 1, S5_LANES), F32)] * 2
        + [jax.ShapeDtypeStruct((depth, S5_GROUP, S5_LANES), F32)] * 2,
        compiler_params=_params(("parallel",)),
        name="s5_params",
    )(lre, lim, ldt, bre, bim)


def _s5_body(*refs, n_batch, n_time, has_state):
    weights = refs[:9]
    refs = refs[9:]
    h0 = None
    if has_state:
        h0 = refs[:2]
        refs = refs[2:]
    sr_ref, si_ref = refs[1:3]

    @pl.when(pl.program_id(0) == 0)
    def _():
        if h0 is not None:
            sr_ref[...] = h0[0][...]
            si_ref[...] = h0[1][...]
        else:
            sr_ref[...] = jnp.zeros_like(sr_ref)
            si_ref[...] = jnp.zeros_like(si_ref)

    _s5_chunk(*weights, *refs, n_batch=n_batch, n_time=n_time)


def _s5_chunk(u_ref, perm_ref, permt_ref, are_ref, aim_ref, bm_ref, cre_ref, cim_ref, d_ref,
              y_ref, sr_ref, si_ref, ub_s, bur_s, bui_s, hr_s, hi_s, y_s, *, n_batch, n_time):
    rows = n_batch * n_time
    n_bt = n_batch // SUBLANES

    u = u_ref[...].reshape(rows, S5_WIDTH)
    ub = _dot(perm_ref[...], u.astype(BF16)).astype(BF16)
    for j in range(S5_BLOCKS):
        ub_s[j] = ub[:, j * S5_BLOCK_IN:(j + 1) * S5_BLOCK_IN]

    def project_in(j):
        r = _dot(ub_s[j], bm_ref[j])
        bur_s[j] = r[:, :S5_BLOCK_ST]
        bui_s[j] = r[:, S5_BLOCK_ST:]

    def scan(j):
        a_re = jnp.broadcast_to(are_ref[j], (SUBLANES, S5_BLOCK_ST))
        a_im = jnp.broadcast_to(aim_ref[j], (SUBLANES, S5_BLOCK_ST))

        def advance(h, row):
            h_re, h_im = h
            tile = slice(row, row + SUBLANES)
            return (a_re * h_re - a_im * h_im + bur_s[j, tile, :],
                    a_re * h_im + a_im * h_re + bui_s[j, tile, :])

        def emit(row, first, second):
            pair = slice(row, row + 2 * SUBLANES)
            hr_s[j, pair, :] = jnp.concatenate([first[0], second[0]], axis=0).astype(BF16)
            hi_s[j, pair, :] = jnp.concatenate([first[1], second[1]], axis=0).astype(BF16)

        if n_bt == 1:
            h = (sr_ref[j], si_ref[j])
            for t in range(0, n_time, 2):
                h1 = advance(h, t * SUBLANES)
                h = advance(h1, (t + 1) * SUBLANES)
                emit(t * SUBLANES, h1, h)
            sr_ref[j], si_ref[j] = h
        else:
            for bt in range(0, n_bt, 2):
                rows_a = slice(bt * SUBLANES, (bt + 1) * SUBLANES)
                rows_b = slice((bt + 1) * SUBLANES, (bt + 2) * SUBLANES)
                ha = (sr_ref[j, rows_a, :], si_ref[j, rows_a, :])
                hb = (sr_ref[j, rows_b, :], si_ref[j, rows_b, :])
                for t in range(n_time):
                    row = t * n_batch + bt * SUBLANES
                    ha = advance(ha, row)
                    hb = advance(hb, row + SUBLANES)
                    emit(row, ha, hb)
                sr_ref[j, rows_a, :], si_ref[j, rows_a, :] = ha
                sr_ref[j, rows_b, :], si_ref[j, rows_b, :] = hb

    def project_out(j):
        y_s[j] = _dot(hr_s[j], cre_ref[j]) - _dot(hi_s[j], cim_ref[j])

    project_in(0)

    def pipelined(j, carry):
        project_in(j + 1)
        scan(j)
        project_out(j)
        return carry

    lax.fori_loop(0, S5_BLOCKS - 1, pipelined, 0)
    scan(S5_BLOCKS - 1)
    project_out(S5_BLOCKS - 1)
    y = jnp.concatenate([y_s[j] for j in range(S5_BLOCKS)], axis=1)
    y_hi = y.astype(BF16)
    y_lo = (y - y_hi.astype(F32)).astype(BF16)
    permt = permt_ref[...]
    y = _dot(permt, y_hi) + _dot(permt, y_lo) + d_ref[...] * u
    y_ref[...] = y.reshape(y_ref.shape)


def _s5(u, perm, permt, a_re, a_im, bm, cre, cim, d, h0, layer, n_batch, n_time):
    seq = u.shape[0] // n_batch
    rows = n_batch * n_time
    has_state = h0 is not None
    assert n_batch % (2 * SUBLANES) == 0 or (n_batch == SUBLANES and n_time % 2 == 0)
    lead, blk_rows = (n_batch, n_time) if seq > n_time else (1, rows)
    u = u.reshape(lead, u.shape[0] // lead, S5_WIDTH)
    u_spec = pl.BlockSpec((lead, blk_rows, S5_WIDTH), lambda c: (0, c, 0))
    state_in_spec = pl.BlockSpec((None, S5_BLOCKS, n_batch, S5_BLOCK_ST), lambda c: (layer, 0, 0, 0))
    state_spec = pl.BlockSpec((S5_BLOCKS, n_batch, S5_BLOCK_ST), lambda c: (0, 0, 0))
    state_shape = jax.ShapeDtypeStruct((S5_BLOCKS, n_batch, S5_BLOCK_ST), F32)
    in_specs = [u_spec,
                _const_spec((rows, rows)), _const_spec((rows, rows)),
                _layer_spec((S5_BLOCKS, 1, S5_BLOCK_ST), layer), _layer_spec((S5_BLOCKS, 1, S5_BLOCK_ST), layer),
                _layer_spec((S5_BLOCKS, S5_BLOCK_IN, 2 * S5_BLOCK_ST), layer),
                _layer_spec((S5_BLOCKS, S5_BLOCK_ST, S5_BLOCK_IN), layer),
                _layer_spec((S5_BLOCKS, S5_BLOCK_ST, S5_BLOCK_IN), layer),
                _layer_spec((1, S5_WIDTH), layer)]
    args = [u, perm, permt, a_re, a_im, bm, cre, cim, d]
    if has_state:
        in_specs += [state_in_spec, state_in_spec]
        args += list(h0)
    return pl.pallas_call(
        functools.partial(_s5_body, n_batch=n_batch, n_time=n_time, has_state=has_state),
        grid=(seq // n_time,),
        in_specs=in_specs,
        out_specs=[u_spec, state_spec, state_spec],
        out_shape=[jax.ShapeDtypeStruct(u.shape, F32), state_shape, state_shape],
        scratch_shapes=[pltpu.VMEM((S5_BLOCKS, rows, S5_BLOCK_IN), BF16),
                        pltpu.VMEM((S5_BLOCKS, rows, S5_BLOCK_ST), F32),
                        pltpu.VMEM((S5_BLOCKS, rows, S5_BLOCK_ST), F32),
                        pltpu.VMEM((S5_BLOCKS, rows, S5_BLOCK_ST), BF16),
                        pltpu.VMEM((S5_BLOCKS, rows, S5_BLOCK_ST), BF16),
                        pltpu.VMEM((S5_BLOCKS, rows, S5_BLOCK_IN), F32)],
        compiler_params=_params(("arbitrary",)),
        name="s5",
    )(*args)


def _time_major_perm(n_batch, n_time):
    r = jnp.arange(n_batch * n_time)
    src = (r % n_batch) * n_time + r // n_batch
    return (src[:, None] == r[None, :]).astype(BF16)


def _mlstm_tile(g, qk_ref, v_ref, gc_ref, cw_ref, cb_ref, bc_ref,
                c_in, n_in, m_in, hm_ref, c_ref, n_ref, m_ref, conv_ref, xp_s, qk_s,
                *, n_sub, seq_len, carry_hist):
    chunk = n_sub * seq_len
    region = SUBLANES + -(-seq_len // SUBLANES) * SUBLANES
    hist = SUBLANES
    slots = [g * n_sub + j for j in range(n_sub)]

    for j, slot in enumerate(slots):
        base = slot * region
        x_raw = qk_ref[g, j * seq_len:(j + 1) * seq_len, :]
        xp_s[base + hist:base + hist + seq_len, :] = x_raw
        if seq_len % SUBLANES == 0:
            x3 = x_raw.reshape(seq_len // SUBLANES, SUBLANES, 2 * MLSTM_WIDTH)
            prev = xp_s[base:base + hist, :]
            sub = lax.broadcasted_iota(jnp.int32, (1, SUBLANES, 1), 1)
            acc = cb_ref[...] + x3 * cw_ref[CONV_HIST:CONV_WIDTH, :]
            for k in range(1, CONV_WIDTH):
                rot = pltpu.roll(x3, k, axis=1)
                before = jnp.concatenate([pltpu.roll(prev, k, axis=0)[None], rot[:-1]], axis=0)
                acc = acc + jnp.where(sub < k, before, rot) * cw_ref[CONV_HIST - k:CONV_WIDTH - k, :]
            acc = acc.reshape(seq_len, 2 * MLSTM_WIDTH)
        else:
            acc = jnp.broadcast_to(cb_ref[...], (seq_len, 2 * MLSTM_WIDTH))
            for i in range(CONV_WIDTH):
                off = base + hist - CONV_HIST + i
                acc = acc + xp_s[off:off + seq_len, :] * cw_ref[i:i + 1, :]
        new_hist = xp_s[base + hist + seq_len - CONV_HIST:base + hist + seq_len, :]
        conv_ref[slot] = new_hist
        if carry_hist:
            xp_s[base + hist - CONV_HIST:base + hist, :] = new_hist
        if n_sub > 1:
            qk_s[g, j * seq_len:(j + 1) * seq_len, :] = acc
        yield
    qk = jax.nn.silu(qk_s[g] if n_sub > 1 else acc)
    q_all = qk[:, :MLSTM_WIDTH] * (MLSTM_HEAD_DIM ** -0.5)
    k_all = qk[:, MLSTM_WIDTH:]
    v_all = v_ref[g]
    yield

    def seq_of(idx):
        s = jnp.zeros(idx.shape, F32)
        for j in range(1, n_sub):
            s = s + jnp.where(idx >= j * seq_len, 1.0, 0.0)
        return s

    row_seq = seq_of(lax.broadcasted_iota(jnp.int32, (chunk, 1), 0))
    col_seq = seq_of(lax.broadcasted_iota(jnp.int32, (1, chunk), 1))
    t_id = lax.broadcasted_iota(jnp.int32, (chunk, chunk), 0)
    s_id = lax.broadcasted_iota(jnp.int32, (chunk, chunk), 1)
    same = jnp.where(row_seq == col_seq, 1.0, 0.0) if n_sub > 1 else jnp.ones((chunk, chunk), F32)
    tril = jnp.where(s_id <= t_id, same, 0.0)
    triu = jnp.where(t_id <= s_id, same, 0.0)
    causal = tril > 0.5
    pick = (lax.broadcasted_iota(jnp.int32, (SUBLANES, LANES), 0)
            == lax.broadcasted_iota(jnp.int32, (SUBLANES, LANES), 1)).astype(F32)

    li_col = gc_ref[g] + bc_ref[...]
    li_row = _dot_nt(pick, li_col, precision=HIGHEST)
    lf_col = jax.nn.log_sigmoid(li_col)
    yield
    lf_row = jax.nn.log_sigmoid(li_row)
    b_col = jnp.dot(tril, lf_col, preferred_element_type=F32, precision=HIGHEST)
    b_row = jnp.dot(lf_row, triu, preferred_element_type=F32, precision=HIGHEST)
    yield

    def per_row(vals):
        out = vals[0]
        for j in range(1, n_sub):
            out = jnp.where(row_seq == j, vals[j], out)
        return out

    m_new_parts = [[] for _ in range(n_sub)]
    for h in range(MLSTM_HEADS):
        hs = slice(h * MLSTM_HEAD_DIM, (h + 1) * MLSTM_HEAD_DIM)
        bc = b_col[:, MLSTM_HEADS + h:MLSTM_HEADS + h + 1]
        br = b_row[MLSTM_HEADS + h:MLSTM_HEADS + h + 1, :]
        lic = li_col[:, h:h + 1]
        lir = li_row[h:h + 1, :]
        m_prev = [m_in[s][:, h:h + 1] for s in slots]
        c_old = [c_in[s, h] for s in slots]
        n_old = [n_in[s, h:h + 1, :] for s in slots]
        qf = q_all[:, hs]
        kf = k_all[:, hs]
        qb = qf.astype(BF16)
        vb = v_all[:, hs].astype(BF16)

        dmat = jnp.where(causal, bc - br + lir, -jnp.inf)
        inter = bc + per_row(m_prev)
        m_t = jnp.maximum(inter, jnp.max(dmat, axis=-1, keepdims=True))
        s_qk = _dot_nt(qb, kf.astype(BF16))
        q_c = per_row([_dot(qb, c.astype(BF16)) for c in c_old])
        yield
        w = jnp.exp(dmat - m_t)
        scores = s_qk * w
        a = jnp.exp(inter - m_t)
        num = _dot(scores.astype(BF16), vb) + a * q_c
        nq = (jnp.sum(scores, axis=-1, keepdims=True)
              + a * jnp.sum(qf * per_row(n_old), axis=-1, keepdims=True))
        yield
        hout = num / jnp.maximum(jnp.abs(nq), jnp.exp(-m_t))
        hm_ref[g, :, hs] = hout
        yield

        for j, slot in enumerate(slots):
            last = (j + 1) * seq_len - 1
            b_last = br[:, last:last + 1]
            g_row = b_last - br + lir
            if n_sub > 1:
                g_row = jnp.where(col_seq == j, g_row, -jnp.inf)
            m_new = jnp.maximum(b_last + m_prev[j], jnp.max(g_row, axis=-1, keepdims=True))
            wk = jnp.exp(b_last - bc + lic - m_new)
            if n_sub > 1:
                wk = jnp.where(row_seq == j, wk, 0.0)
            decay = jnp.exp(b_last + m_prev[j] - m_new)
            kw = kf * wk
            c_ref[slot, h] = decay * c_old[j] + _dot_tn(kw.astype(BF16), vb)
            n_ref[slot, h:h + 1, :] = decay * n_old[j] + jnp.sum(kw, axis=0, keepdims=True)
            m_new_parts[j].append(m_new)
            yield
    for j, slot in enumerate(slots):
        m_ref[slot] = jnp.concatenate(m_new_parts[j], axis=1)


def _mlstm_body(*refs, n_sub, seq_len, has_state, n_prev, single_chunk, par):
    qk_ref, v_ref, gc_ref, cw_ref, cb_ref, bc_ref = refs[:6]
    refs = refs[6:]
    c0_ref = n0_ref = m0_ref = conv0_ref = None
    if has_state:
        c0_ref, n0_ref, m0_ref, conv0_ref = refs[:4]
        refs = refs[4:]
    hm_ref, c_ref, n_ref, m_ref, conv_ref = refs[n_prev:n_prev + 5]
    scratch = refs[n_prev + 5:]
    xp_s = scratch[0]
    qk_s = scratch[1] if n_sub > 1 else None

    from_input = has_state and single_chunk
    c_in, n_in, m_in = (c0_ref, n0_ref, m0_ref) if from_input else (c_ref, n_ref, m_ref)

    @pl.when(pl.program_id(1) == 0)
    def _():
        _mlstm_init(c_ref, n_ref, m_ref, xp_s, None if from_input else (c0_ref, n0_ref, m0_ref), conv0_ref,
                    state_from_input=from_input)

    tiles = [_mlstm_tile(g, qk_ref, v_ref, gc_ref, cw_ref, cb_ref, bc_ref,
                         c_in, n_in, m_in, hm_ref, c_ref, n_ref, m_ref, conv_ref, xp_s, qk_s,
                         n_sub=n_sub, seq_len=seq_len, carry_hist=not single_chunk) for g in range(par)]
    order = itertools.zip_longest(*tiles) if single_chunk else itertools.chain(*tiles)
    for _ in order:
        pass


def _mlstm_init(c_ref, n_ref, m_ref, xp_s, state0, conv0_ref, *, state_from_input):
    n_slots = c_ref.shape[0]
    region = xp_s.shape[0] // n_slots
    for slot in range(n_slots):
        xp_s[slot * region:slot * region + SUBLANES, :] = jnp.zeros((SUBLANES, 2 * MLSTM_WIDTH), F32)
        if conv0_ref is not None:
            xp_s[slot * region + SUBLANES - CONV_HIST:slot * region + SUBLANES, :] = conv0_ref[slot]
    if state_from_input:
        return
    for ref, ref0 in zip((c_ref, n_ref, m_ref), state0):
        ref[...] = jnp.zeros_like(ref) if ref0 is None else ref0[...]


def _mlstm(qk, v, gc, cw, cb, bias_c, state, prev, layer, depth, n_tiles, n_sub, seq_len, n_chunk, par):
    assert n_sub == 1 or n_chunk == 1
    assert n_tiles % par == 0
    hh, dh = MLSTM_HEADS, MLSTM_HEAD_DIM
    chunk = n_sub * seq_len
    n_slots = par * n_sub
    n_all = n_tiles * n_sub
    has_state = state is not None
    tile3 = lambda t: t.reshape(n_tiles, n_chunk * chunk, t.shape[-1])
    act_spec = lambda w: pl.BlockSpec((par, chunk, w), lambda b, c: (b, c, 0))
    st5 = lambda b, c: (layer, b, 0, 0, 0)
    st4 = lambda b, c: (layer, b, 0, 0)
    state_specs = [pl.BlockSpec((None, n_slots, hh, dh, dh), st5),
                   pl.BlockSpec((None, n_slots, hh, dh), st4),
                   pl.BlockSpec((None, n_slots, 1, hh), st4),
                   pl.BlockSpec((None, n_slots, CONV_HIST, 2 * MLSTM_WIDTH), st4)]
    in_specs = [act_spec(2 * MLSTM_WIDTH), act_spec(MLSTM_WIDTH), act_spec(LANES),
                _layer_spec((CONV_WIDTH, 2 * MLSTM_WIDTH), layer), _layer_spec((1, 2 * MLSTM_WIDTH), layer),
                _layer_spec((1, LANES), layer)]
    args = [tile3(qk), tile3(v), tile3(gc), cw, cb, bias_c]
    if has_state:
        in_specs += state_specs
        args += list(state)
    aliases = {len(args) + k: 1 + k for k in range(len(prev))}
    in_specs += [_ANY] * len(prev)
    args += list(prev)
    out_shape = [jax.ShapeDtypeStruct((n_tiles, n_chunk * chunk, MLSTM_WIDTH), F32),
                 jax.ShapeDtypeStruct((depth, n_all, hh, dh, dh), F32),
                 jax.ShapeDtypeStruct((depth, n_all, hh, dh), F32),
                 jax.ShapeDtypeStruct((depth, n_all, 1, hh), F32),
                 jax.ShapeDtypeStruct((depth, n_all, CONV_HIST, 2 * MLSTM_WIDTH), F32)]
    region = SUBLANES + -(-seq_len // SUBLANES) * SUBLANES
    scratch = [pltpu.VMEM((n_slots * region, 2 * MLSTM_WIDTH), F32)]
    if n_sub > 1:
        scratch.append(pltpu.VMEM((par, chunk, 2 * MLSTM_WIDTH), F32))
    return pl.pallas_call(
        functools.partial(_mlstm_body, n_sub=n_sub, seq_len=seq_len, has_state=has_state,
                          n_prev=len(prev), single_chunk=n_chunk == 1, par=par),
        grid=(n_tiles // par, n_chunk),
        in_specs=in_specs,
        out_specs=[act_spec(MLSTM_WIDTH)] + state_specs,
        out_shape=out_shape,
        scratch_shapes=scratch,
        input_output_aliases=aliases,
        compiler_params=_params(("parallel", "arbitrary")),
        name="mlstm",
    )(*args)


def _post_body(x_ref, y_ref, hm_ref, g_ref, gm_ref, wo_ref, wgs_ref, wgm_ref, wglu_ref, wsu_ref, wmu_ref,
               wout_ref, o_ref):
    x = x_ref[...]
    h = _rms(x, g_ref[...]).astype(BF16)
    ys = jax.nn.gelu(y_ref[...])
    ys = ys * jax.nn.sigmoid(_dot(ys.astype(BF16), wglu_ref[...]))
    heads = []
    for k in range(MLSTM_HEADS):
        hk = hm_ref[:, k * MLSTM_HEAD_DIM:(k + 1) * MLSTM_HEAD_DIM]
        heads.append(hk * lax.rsqrt(jnp.mean(hk * hk, axis=-1, keepdims=True) + EPS))
    hm = jnp.concatenate(heads, axis=1) * gm_ref[...] * jax.nn.sigmoid(_dot(h, wo_ref[...]))
    merged = (jax.nn.sigmoid(_dot(h, wgs_ref[...])) * _dot(ys.astype(BF16), wsu_ref[...])
              + jax.nn.sigmoid(_dot(h, wgm_ref[...])) * _dot(hm.astype(BF16), wmu_ref[...]))
    o_ref[...] = x + _dot(merged.astype(BF16), wout_ref[...])


def _post(x, y, hm, g, gm, wo, wgs, wgm, wglu, wsu, wmu, wout, layer):
    rows = x.shape[0]
    tm = min(ROW_TILE, rows)
    rb = lambda i: (i, 0)
    sq = _layer_spec((D_MODEL, D_MODEL), layer)
    return pl.pallas_call(
        _post_body,
        grid=(rows // tm,),
        in_specs=[pl.BlockSpec((tm, D_MODEL), rb), pl.BlockSpec((tm, S5_WIDTH), rb),
                  pl.BlockSpec((tm, MLSTM_WIDTH), rb),
                  _layer_spec((1, D_MODEL), layer), _layer_spec((1, MLSTM_WIDTH), layer), sq, sq, sq,
                  _layer_spec((S5_WIDTH, S5_WIDTH), layer), _layer_spec((S5_WIDTH, D_MODEL), layer), sq, sq],
        out_specs=pl.BlockSpec((tm, D_MODEL), rb),
        out_shape=jax.ShapeDtypeStruct((rows, D_MODEL), F32),
        compiler_params=_params(("parallel",)),
        name="post",
    )(x, y, hm, g, gm, wo, wgs, wgm, wglu, wsu, wmu, wout)


N_POST_INPUTS = 12


def _post_ffn_ple_body(*refs, final):
    post_in, ple_in = refs[:N_POST_INPUTS], refs[N_POST_INPUTS:-2]
    o_ref, x_s = refs[-2:]
    _post_body(*post_in, x_s)
    _ffn_ple_body(x_s, *ple_in, o_ref, final=final)


def _post_ffn_ple(x, y, hm, g, gm, wo, wgs, wgm, wglu, wsu, wmu, wout,
                  p, g2, wg, wu, wd, gp, wp, wpg, gf, layer, final):
    rows = x.shape[0]
    tm = min(FUSED_ROW_TILE, rows)
    rb = lambda i: (i, 0)
    sq = _layer_spec((D_MODEL, D_MODEL), layer)
    return pl.pallas_call(
        functools.partial(_post_ffn_ple_body, final=final),
        grid=(rows // tm,),
        in_specs=[pl.BlockSpec((tm, D_MODEL), rb), pl.BlockSpec((tm, S5_WIDTH), rb),
                  pl.BlockSpec((tm, MLSTM_WIDTH), rb),
                  _layer_spec((1, D_MODEL), layer), _layer_spec((1, MLSTM_WIDTH), layer), sq, sq, sq,
                  _layer_spec((S5_WIDTH, S5_WIDTH), layer), _layer_spec((S5_WIDTH, D_MODEL), layer), sq, sq,
                  pl.BlockSpec((None, tm, D_PLE), lambda i: (layer, i, 0)),
                  _layer_spec((1, D_MODEL), layer), _layer_spec((D_MODEL, D_FF), layer),
                  _layer_spec((D_MODEL, D_FF), layer), _layer_spec((D_FF, D_MODEL), layer),
                  _layer_spec((1, D_MODEL), layer), _layer_spec((D_PLE, D_MODEL), layer), sq,
                  _const_spec((1, D_MODEL))],
        out_specs=pl.BlockSpec((tm, D_MODEL), rb),
        out_shape=jax.ShapeDtypeStruct((rows, D_MODEL), F32),
        scratch_shapes=[pltpu.VMEM((tm, D_MODEL), F32)],
        compiler_params=_params(("parallel",)),
        name="post_ffn_ple",
    )(x, y, hm, g, gm, wo, wgs, wgm, wglu, wsu, wmu, wout, p, g2, wg, wu, wd, gp, wp, wpg, gf)


def _block_diag(t):
    depth, nb, gpb, a, c = t.shape
    eye = jnp.eye(gpb, dtype=t.dtype)
    return jnp.einsum("ljgac,gk->ljgakc", t, eye).reshape(depth, nb, gpb * a, gpb * c)


def kernel(x_prompt, x_sample, state_s5_re, state_s5_im, state_mlstm_c, state_mlstm_n, state_mlstm_m, state_conv, p_prompt, p_sample, g_ffn1, w1_gate, w1_up, w1_down, g_mix, w_in, s5_lambda_re, s5_lambda_im, s5_log_dt, s5_b_re, s5_b_im, s5_c_re, s5_c_im, s5_d, s5_w_glu, w_s5_up, conv_w, conv_b, b_igate, b_fgate, g_mhead, w_m_up, w_out, g_ffn2, w2_gate, w2_up, w2_down, g_ple, w_ple, w_ple_gate, g_final):
    depth = w_in.shape[0]
    bp, sp, _ = x_prompt.shape
    bs, ss, _ = x_sample.shape
    hh = MLSTM_HEADS
    gpb = S5_GROUPS // S5_BLOCKS

    bf = lambda w: w.astype(BF16)
    cuts = [0, S5_WIDTH, S5_WIDTH + 2 * MLSTM_WIDTH, S5_WIDTH + 3 * MLSTM_WIDTH, S5_WIDTH + 4 * MLSTM_WIDTH]
    c_gate = cuts[4] + 2 * MLSTM_HEADS
    w_u = bf(w_in[:, :, cuts[0]:cuts[1]])
    w_qk = bf(w_in[:, :, cuts[1]:cuts[2]])
    w_v = bf(w_in[:, :, cuts[2]:cuts[3]])
    w_o = bf(w_in[:, :, cuts[3]:cuts[4]])
    w_gc = bf(jnp.pad(w_in[:, :, cuts[4]:c_gate], ((0, 0), (0, 0), (0, LANES - 2 * MLSTM_HEADS))))
    w_gs = bf(w_in[:, :, c_gate:c_gate + D_MODEL])
    w_gm = bf(w_in[:, :, c_gate + D_MODEL:])
    w1g, w1u, w1d = bf(w1_gate), bf(w1_up), bf(w1_down)
    w2g, w2u, w2d = bf(w2_gate), bf(w2_up), bf(w2_down)
    wglu, wsu, wmu, wout = bf(s5_w_glu), bf(w_s5_up), bf(w_m_up), bf(w_out)
    wple, wpg = bf(w_ple), bf(w_ple_gate)
    row = lambda g: g.reshape(depth, 1, -1)
    gf1, gmx, gf2, gpl, gmh, s5d, cvb = (row(g_ffn1), row(g_mix), row(g_ffn2), row(g_ple),
                                          row(g_mhead), row(s5_d), row(conv_b))
    gfin = g_final.reshape(1, D_MODEL)
    bias_c = jnp.pad(jnp.concatenate([b_igate, b_fgate], axis=1),
                     ((0, 0), (0, LANES - 2 * MLSTM_HEADS))).reshape(depth, 1, LANES)

    lanes3 = lambda t: t.reshape(depth, 1, S5_LANES)
    ldt = jnp.broadcast_to(s5_log_dt[:, :, None], (depth, S5_GROUPS, S5_STATE))
    to_cols = lambda t: jnp.transpose(t, (0, 3, 1, 2)).reshape(depth, S5_GROUP, S5_LANES)
    a_re, a_im, bb_re, bb_im = _s5_params(lanes3(s5_lambda_re), lanes3(s5_lambda_im), lanes3(ldt),
                                          to_cols(s5_b_re), to_cols(s5_b_im))
    to_blk = lambda t: jnp.transpose(t.reshape(depth, S5_GROUP, S5_BLOCKS, gpb, S5_STATE), (0, 2, 3, 1, 4))
    bmat = bf(jnp.concatenate([_block_diag(to_blk(bb_re)), _block_diag(to_blk(bb_im))], axis=-1))
    c_blk = lambda t: jnp.transpose(t.reshape(depth, S5_BLOCKS, gpb, S5_GROUP, S5_STATE), (0, 1, 2, 4, 3))
    cmat_re = bf(_block_diag(c_blk(s5_c_re)))
    cmat_im = bf(_block_diag(c_blk(s5_c_im)))
    s5_time_p = S5_ROWS // bp
    perm_p = _time_major_perm(bp, s5_time_p)
    perm_s = _time_major_perm(bs, ss)

    xp = x_prompt.reshape(bp * sp, D_MODEL)
    xs = x_sample.reshape(bs * ss, D_MODEL)
    pp = p_prompt.reshape(depth, bp * sp, D_PLE)
    ps = p_sample.reshape(depth, bs * ss, D_PLE)
    to_blocks = lambda t: jnp.swapaxes(t.reshape(depth, -1, S5_BLOCKS, S5_BLOCK_ST), 1, 2)
    from_blocks = lambda t: jnp.swapaxes(t, 1, 2).reshape(depth, -1, S5_GROUPS, S5_STATE)
    a_re = a_re.reshape(depth, S5_BLOCKS, 1, S5_BLOCK_ST)
    a_im = a_im.reshape(depth, S5_BLOCKS, 1, S5_BLOCK_ST)
    s5_state = (to_blocks(state_s5_re), to_blocks(state_s5_im))
    lstm_state = (state_mlstm_c, state_mlstm_n, state_mlstm_m.reshape(depth, bs, 1, hh), state_conv)

    s5_out_p, s5_out_s, lstm_out_p, lstm_out_s = [], [], [], []
    for i in range(depth):
        final = i == depth - 1
        ffn1_w = (gf1, w1g, w1u, w1d)
        proj_w = (gmx, w_u, w_qk, w_v, w_gc)
        s5_w = (a_re, a_im, bmat, cmat_re, cmat_im, s5d)
        lstm_w = (conv_w, cvb, bias_c)
        mix_w = (gmx, gmh, w_o, w_gs, w_gm, wglu, wsu, wmu, wout)
        ple_w = (gf2, w2g, w2u, w2d, gpl, wple, wpg, gfin)

        xp = _ffn(xp, *ffn1_w, layer=i)
        u, qk, v, gc = _inproj(xp, *proj_w, layer=i, v_dtype=BF16)
        y, s5_re, s5_im = _s5(u, perm_p, perm_p.T, *s5_w, None, layer=i, n_batch=bp, n_time=s5_time_p)
        s5_out_p.append((s5_re, s5_im))
        hm, *lstm_out_p = _mlstm(qk, v, gc, *lstm_w, None, lstm_out_p, layer=i, depth=depth,
                                 n_tiles=bp, n_sub=1, seq_len=MLSTM_CHUNK, n_chunk=sp // MLSTM_CHUNK,
                                 par=PROMPT_TILES_PER_STEP)
        xp = _post(xp, y.reshape(bp * sp, S5_WIDTH), hm.reshape(bp * sp, MLSTM_WIDTH), *mix_w, layer=i)
        xp = _ffn_ple(xp, pp, *ple_w, layer=i, final=final)

        xs, u, qk, v, gc = _ffn_inproj(xs, *ffn1_w, *proj_w, layer=i, v_dtype=F32)
        y, s5_re, s5_im = _s5(u, perm_s, perm_s.T, *s5_w, s5_state, layer=i, n_batch=bs, n_time=ss)
        s5_out_s.append((s5_re, s5_im))
        hm, *lstm_out_s = _mlstm(qk, v, gc, *lstm_w, lstm_state, lstm_out_s, layer=i, depth=depth,
                                 n_tiles=bs // SAMPLE_SEQS_PER_TILE, n_sub=SAMPLE_SEQS_PER_TILE,
                                 seq_len=ss, n_chunk=1, par=SAMPLE_TILES_PER_STEP)
        xs = _post_ffn_ple(xs, y.reshape(bs * ss, S5_WIDTH), hm.reshape(bs * ss, MLSTM_WIDTH), *mix_w,
                           ps, *ple_w, layer=i, final=final)

    def states(n, s5_out, lstm_out):
        c_n, n_n, m_n, conv_n = lstm_out
        s5_re, s5_im = (jnp.stack(parts) for parts in zip(*s5_out))
        return (from_blocks(s5_re), from_blocks(s5_im), c_n, n_n, m_n.reshape(depth, n, hh), conv_n)

    return ((xp.reshape(bp, sp, D_MODEL), xs.reshape(bs, ss, D_MODEL))
            + states(bp, s5_out_p, lstm_out_p) + states(bs, s5_out_s, lstm_out_s))
```
